```python
import math
import jax, jax.numpy as jnp
from jax import lax
import numpy as np

D_MODEL = 2048
BATCH = 4
SEQ = 4096
DEPTH = 2

N_MIXERS = 2
N_A = (DEPTH + 1) // 2
N_B = DEPTH // 2
EPS = 1e-5

A_WIDTH = D_MODEL
A_GROUPS = 16
A_GROUP_DIM = A_WIDTH // A_GROUPS
CHUNK = 128

B_HEADS = 16
B_QK_DIM = D_MODEL // (2 * B_HEADS)
B_V_DIM = 2 * B_QK_DIM
ROT_DIM = B_QK_DIM // 4
ROPE_THETA = 500000.0
Q_BLOCK = 128

N_EXPERTS = 32
TOP_K = 4
D_EXPERT = D_MODEL
SWIGLU_LIMIT = 7.0
SWIGLU_ALPHA = 1.702
MOE_BLOCK = 128

kernel_name = "hybrid_gmlp_diffattn_moe"


def rms_norm(x, g):
    xf = x.astype(jnp.float32)
    y = xf * lax.rsqrt(jnp.mean(xf * xf, axis=-1, keepdims=True) + EPS)
    return (y * g.astype(jnp.float32)).astype(x.dtype)


def layer_norm(x, g, b):
    xf = x.astype(jnp.float32)
    mu = jnp.mean(xf, axis=-1, keepdims=True)
    xc = xf - mu
    y = xc * lax.rsqrt(jnp.mean(xc * xc, axis=-1, keepdims=True) + EPS)
    return (y * g.astype(jnp.float32) + b.astype(jnp.float32)).astype(x.dtype)


def rope_tables(seq):
    inv_freq = ROPE_THETA ** (-jnp.arange(0, ROT_DIM, 2, dtype=jnp.float32) / ROT_DIM)
    ang = jnp.arange(seq, dtype=jnp.float32)[:, None] * inv_freq[None, :]
    return jnp.cos(ang), jnp.sin(ang)


def apply_partial_rope(x, cos, sin):
    xr = x[..., :ROT_DIM].astype(jnp.float32)
    half = ROT_DIM // 2
    x1, x2 = xr[..., :half], xr[..., half:]
    c = cos[None, :, None, None, :]
    s = sin[None, :, None, None, :]
    rot = jnp.concatenate([x1 * c - x2 * s, x2 * c + x1 * s], axis=-1).astype(x.dtype)
    return jnp.concatenate([rot, x[..., ROT_DIM:]], axis=-1)


def chunked_gmlp(h, w_in, ln_g, ln_b, w_s, b_s, w_out):
    B, S, _ = h.shape
    z = jax.nn.gelu(h @ w_in)
    u, v = z[..., :A_WIDTH], z[..., A_WIDTH:]
    v = layer_norm(v, ln_g, ln_b)
    v = v.reshape(B, S // CHUNK, CHUNK, A_GROUPS, A_GROUP_DIM)
    causal = jnp.tril(jnp.ones((CHUNK, CHUNK), dtype=bool))
    w = jnp.where(causal, w_s, 0.0).astype(v.dtype)
    sv = jnp.einsum('gts,bnsgc->bntgc', w, v) + b_s.T[:, :, None].astype(v.dtype)
    y = u * sv.reshape(B, S, A_WIDTH)
    return y @ w_out


def diff_attention(h, w_qkv, lq1, lk1, lq2, lk2, subln_g, w_o, lambda_init, cos, sin):
    B, S, _ = h.shape
    dq = B_HEADS * 2 * B_QK_DIM
    qkv = h @ w_qkv
    q = qkv[..., :dq].reshape(B, S, B_HEADS, 2, B_QK_DIM)
    k = qkv[..., dq:2 * dq].reshape(B, S, B_HEADS, 2, B_QK_DIM)
    v = qkv[..., 2 * dq:].reshape(B, S, B_HEADS, B_V_DIM)
    q = apply_partial_rope(q, cos, sin) * (B_QK_DIM ** -0.5)
    k = apply_partial_rope(k, cos, sin)
    f32 = jnp.float32
    lam = (jnp.exp(jnp.sum(lq1.astype(f32) * lk1.astype(f32)))
           - jnp.exp(jnp.sum(lq2.astype(f32) * lk2.astype(f32))) + lambda_init)
    nq = S // Q_BLOCK
    q_blocks = q.reshape(B, nq, Q_BLOCK, B_HEADS, 2, B_QK_DIM).transpose(1, 0, 3, 4, 2, 5)
    k_t = k.transpose(0, 2, 3, 1, 4)
    v_t = v.transpose(0, 2, 1, 3)
    key_pos = jnp.arange(S)

    def attend(args):
        qb, start = args
        s = jnp.einsum('bhcqd,bhckd->bhcqk', qb, k_t).astype(f32)
        causal = key_pos[None, :] <= (start + jnp.arange(Q_BLOCK))[:, None]
        p = jax.nn.softmax(jnp.where(causal, s, -jnp.inf), axis=-1)
        a = p[:, :, 0] - lam * p[:, :, 1]
        return jnp.einsum('bhqk,bhkv->bhqv', a.astype(v_t.dtype), v_t)

    o = lax.map(attend, (q_blocks, jnp.arange(nq) * Q_BLOCK))
    o = o.transpose(1, 0, 3, 2, 4).reshape(B, S, B_HEADS, B_V_DIM)
    o = rms_norm(o, subln_g) * (1.0 - lambda_init)
    return o.reshape(B, S, B_HEADS * B_V_DIM) @ w_o


def moe(h, router_w, router_b, w_gu, b_gu, w_down, b_down):
    T, D = h.shape
    TK = T * TOP_K
    logits = (h @ router_w + router_b).astype(jnp.float32)
    top_v, top_i = lax.top_k(logits, TOP_K)
    gates = jax.nn.softmax(top_v, axis=-1)
    flat_e = top_i.reshape(-1)
    flat_tok = jnp.repeat(jnp.arange(T, dtype=jnp.int32), TOP_K)
    order = jnp.argsort(flat_e)
    sorted_e = flat_e[order]
    counts = jnp.bincount(flat_e, length=N_EXPERTS)
    padded = ((counts + MOE_BLOCK - 1) // MOE_BLOCK) * MOE_BLOCK
    pad_end = jnp.cumsum(padded)
    pad_start = pad_end - padded
    grp_start = jnp.cumsum(counts) - counts
    dest = pad_start[sorted_e] + (jnp.arange(TK) - grp_start[sorted_e])
    n_blocks = TK // MOE_BLOCK + N_EXPERTS
    P = n_blocks * MOE_BLOCK
    slot_tok = jnp.full((P,), T, dtype=jnp.int32).at[dest].set(flat_tok[order])
    slot_gate = jnp.zeros((P,), h.dtype).at[dest].set(gates.reshape(-1)[order].astype(h.dtype))
    block_expert = jnp.clip(jnp.searchsorted(pad_end, jnp.arange(n_blocks) * MOE_BLOCK, side='right'),
                            0, N_EXPERTS - 1)
    h_pad = jnp.concatenate([h, jnp.zeros((1, D), h.dtype)], axis=0)
    xs = h_pad[slot_tok].reshape(n_blocks, MOE_BLOCK, D)

    def expert_block(args):
        xb, e = args
        gu = xb @ w_gu[e] + b_gu[e]
        g = jnp.minimum(gu[:, ::2], SWIGLU_LIMIT)
        u = jnp.clip(gu[:, 1::2], -SWIGLU_LIMIT, SWIGLU_LIMIT)
        act = g * jax.nn.sigmoid(SWIGLU_ALPHA * g) * (u + 1.0)
        return act @ w_down[e] + b_down[e]

    y = lax.map(expert_block, (xs, block_expert)).reshape(P, D) * slot_gate[:, None]
    return jnp.zeros((T + 1, D), h.dtype).at[slot_tok].add(y)[:T]


def setup_inputs(seed: int = 0) -> dict:
    key = jax.random.key(seed)
    ks = jax.random.split(key, 24)
    f32 = jnp.float32

    def nrm(k, shape, scale):
        return jax.random.normal(k, shape, f32) * scale

    D, F, E = D_MODEL, D_EXPERT, N_EXPERTS
    return {
        "x": nrm(ks[0], (BATCH, SEQ, D), 1.0),
        "attn_norm_g": 1.0 + nrm(ks[1], (DEPTH, D), 0.02),
        "ffn_norm_g": 1.0 + nrm(ks[2], (DEPTH, D), 0.02),
        "final_norm_g": 1.0 + nrm(ks[3], (D,), 0.02),
        "a_w_in": nrm(ks[4], (N_A, D, 2 * A_WIDTH), D ** -0.5),
        "a_ln_g": 1.0 + nrm(ks[5], (N_A, A_WIDTH), 0.02),
        "a_ln_b": nrm(ks[6], (N_A, A_WIDTH), 0.02),
        "a_w_s": nrm(ks[7], (N_A, A_GROUPS, CHUNK, CHUNK), CHUNK ** -0.5),
        "a_b_s": 1.0 + nrm(ks[8], (N_A, A_GROUPS, CHUNK), 0.1),
        "a_w_out": nrm(ks[9], (N_A, A_WIDTH, D), A_WIDTH ** -0.5),
        "b_w_qkv": nrm(ks[10], (N_B, D, 3 * D), D ** -0.5),
        "b_lambda_q1": nrm(ks[11], (N_B, B_QK_DIM), 0.1),
        "b_lambda_k1": nrm(ks[12], (N_B, B_QK_DIM), 0.1),
        "b_lambda_q2": nrm(ks[13], (N_B, B_QK_DIM), 0.1),
        "b_lambda_k2": nrm(ks[14], (N_B, B_QK_DIM), 0.1),
        "b_subln_g": 1.0 + nrm(ks[15], (N_B, B_V_DIM), 0.02),
        "b_w_o": nrm(ks[16], (N_B, B_HEADS * B_V_DIM, D), (B_HEADS * B_V_DIM) ** -0.5),
        "router_w": nrm(ks[17], (DEPTH, D, E), D ** -0.5),
        "router_b": nrm(ks[18], (DEPTH, E), 0.01),
        "w_gate_up": nrm(ks[19], (DEPTH, E, D, 2 * F), D ** -0.5),
        "b_gate_up": nrm(ks[20], (DEPTH, E, 2 * F), 0.01),
        "w_down": nrm(ks[21], (DEPTH, E, F, D), F ** -0.5),
        "b_down": nrm(ks[22], (DEPTH, E, D), 0.01),
    }


def reference(x, attn_norm_g, ffn_norm_g, final_norm_g, a_w_in, a_ln_g, a_ln_b, a_w_s, a_b_s, a_w_out,
              b_w_qkv, b_lambda_q1, b_lambda_k1, b_lambda_q2, b_lambda_k2, b_subln_g, b_w_o,
              router_w, router_b, w_gate_up, b_gate_up, w_down, b_down):
    B, S, D = x.shape
    cos, sin = rope_tables(S)
    for i in range(DEPTH):
        j = i // N_MIXERS
        h = rms_norm(x, attn_norm_g[i])
        if i % N_MIXERS == 0:
            mix = chunked_gmlp(h, a_w_in[j], a_ln_g[j], a_ln_b[j], a_w_s[j], a_b_s[j], a_w_out[j])
        else:
            lambda_init = 0.8 - 0.6 * math.exp(-0.3 * i)
            mix = diff_attention(h, b_w_qkv[j], b_lambda_q1[j], b_lambda_k1[j], b_lambda_q2[j],
                                 b_lambda_k2[j], b_subln_g[j], b_w_o[j], lambda_init, cos, sin)
        x = x + mix
        h = rms_norm(x, ffn_norm_g[i])
        x = x + moe(h.reshape(B * S, D), router_w[i], router_b[i], w_gate_up[i], b_gate_up[i],
                    w_down[i], b_down[i]).reshape(B, S, D)
    return rms_norm(x, final_norm_g)
```

```python
import functools
import math

import jax
import jax.numpy as jnp
from jax import lax
from jax.experimental import pallas as pl
from jax.experimental.pallas import tpu as pltpu

F32 = jnp.float32
BF16 = jnp.bfloat16
I32 = jnp.int32
U32 = jnp.uint32

EPS = 1e-5
TOP_K = 4
ROPE_THETA = 500000.0
SWIGLU_LIMIT = 7.0
SWIGLU_ALPHA = 1.702

V7X_LANES = 128
V7X_VMEM_BYTES = 64 * 1024 * 1024
VMEM_LIMIT_BYTES = V7X_VMEM_BYTES * 7 // 8
HI16 = 0xFFFF0000


def _tile(dim, want):
    t = min(dim, want)
    while dim % t:
        t //= 2
    return t


def _params(semantics):
    return pltpu.CompilerParams(dimension_semantics=semantics, vmem_limit_bytes=VMEM_LIMIT_BYTES)


def _rms_scale(x):
    return lax.rsqrt(jnp.mean(x * x, axis=-1, keepdims=True) + EPS)


def _norm_matmul_gelu_kernel(x_ref, g_ref, w_ref, o_ref, xn_ref):
    @pl.when(pl.program_id(1) == 0)
    def _():
        x = x_ref[...]
        xn_ref[...] = (x * _rms_scale(x) * g_ref[...]).astype(BF16)

    y = jnp.dot(xn_ref[...], w_ref[...], preferred_element_type=F32)
    o_ref[...] = jax.nn.gelu(y).astype(o_ref.dtype)


def _norm_matmul_rope_kernel(x_ref, g_ref, w_ref, tab_ref, o_ref, xn_ref, *, n_rope_tiles, shift):
    j = pl.program_id(1)

    @pl.when(j == 0)
    def _():
        x = x_ref[...]
        xn_ref[...] = (x * _rms_scale(x) * g_ref[...]).astype(BF16)

    y = jnp.dot(xn_ref[...], w_ref[...], preferred_element_type=F32)

    @pl.when(j < n_rope_tiles)
    def _():
        c, s1, s2 = tab_ref[0], tab_ref[1], tab_ref[2]
        for h in range(y.shape[1] // V7X_LANES):
            cols = slice(h * V7X_LANES, (h + 1) * V7X_LANES)
            yh = y[:, cols]
            r = (yh * c + pltpu.roll(yh, shift, 1) * s1
                 + pltpu.roll(yh, V7X_LANES - shift, 1) * s2)
            o_ref[:, cols] = r.astype(o_ref.dtype)

    @pl.when(j >= n_rope_tiles)
    def _():
        o_ref[...] = y.astype(o_ref.dtype)


def _norm_matmul(x, g, w, *, rope_tab=None, seq=None, n_rope_cols=0, rot_half=0):
    t, k = x.shape
    n = w.shape[1]
    tm = _tile(t if seq is None else seq, 1024)
    tn = _tile(n if rope_tab is None else n_rope_cols // 2, 1024)
    grid = (t // tm, n // tn)
    in_specs = [
        pl.BlockSpec((tm, k), lambda i, j: (i, 0)),
        pl.BlockSpec((1, k), lambda i, j: (0, 0)),
        pl.BlockSpec((k, tn), lambda i, j: (0, j)),
    ]
    args = [x, g.reshape(1, k), w]
    if rope_tab is None:
        body = _norm_matmul_gelu_kernel
    else:
        assert n_rope_cols % (2 * tn) == 0 and tn % V7X_LANES == 0
        n_rope_tiles = n_rope_cols // tn
        s_tiles = seq // tm
        in_specs.append(pl.BlockSpec(
            (None, 3, tm, V7X_LANES),
            lambda i, j: (jnp.minimum(j // (n_rope_tiles // 2), 1), 0, i % s_tiles, 0)))
        args.append(rope_tab)
        body = functools.partial(_norm_matmul_rope_kernel, n_rope_tiles=n_rope_tiles, shift=rot_half)
    return pl.pallas_call(
        body,
        grid=grid,
        in_specs=in_specs,
        out_specs=pl.BlockSpec((tm, tn), lambda i, j: (i, j)),
        out_shape=jax.ShapeDtypeStruct((t, n), BF16),
        scratch_shapes=[pltpu.VMEM((tm, k), BF16)],
        compiler_params=_params(("parallel", "arbitrary")),
        name="norm_matmul_gelu" if rope_tab is None else "norm_matmul_rope",
    )(*args)


def _matmul_residual_kernel(a_ref, w_ref, r_ref, o_ref):
    o_ref[...] = r_ref[...] + jnp.dot(a_ref[...], w_ref[...], preferred_element_type=F32)


def _matmul_residual(a, w, res):
    t, k = a.shape
    n = w.shape[1]
    tm, tn = _tile(t, 1024), _tile(n, 1024)
    return pl.pallas_call(
        _matmul_residual_kernel,
        grid=(t // tm, n // tn),
        in_specs=[
            pl.BlockSpec((tm, k), lambda i, j: (i, 0)),
            pl.BlockSpec((k, tn), lambda i, j: (0, j)),
            pl.BlockSpec((tm, tn), lambda i, j: (i, j)),
        ],
        out_specs=pl.BlockSpec((tm, tn), lambda i, j: (i, j)),
        out_shape=jax.ShapeDtypeStruct((t, n), F32),
        compiler_params=_params(("parallel", "parallel")),
        name="matmul_residual",
    )(a, w, res)


def _gmlp_gate_kernel(u_ref, v_ref, lng_ref, lnb_ref, ws_ref, bst_ref, y_ref, vn_ref):
    n_groups, chunk, _ = ws_ref.shape
    tm, width = u_ref.shape
    gd = width // n_groups
    v = v_ref[...].astype(F32)
    vc = v - jnp.mean(v, axis=-1, keepdims=True)
    inv = lax.rsqrt(jnp.mean(vc * vc, axis=-1, keepdims=True) + EPS)
    vn_ref[...] = (vc * inv * lng_ref[...] + lnb_ref[...]).astype(BF16)
    causal = (lax.broadcasted_iota(I32, (chunk, chunk), 0) >= lax.broadcasted_iota(I32, (chunk, chunk), 1))
    for g in range(n_groups):
        w = jnp.where(causal, ws_ref[g], 0.0).astype(BF16)
        bias = bst_ref[:, g:g + 1]
        cols = slice(g * gd, (g + 1) * gd)
        for ci in range(tm // chunk):
            rows = slice(ci * chunk, (ci + 1) * chunk)
            sv = jnp.dot(w, vn_ref[rows, cols], preferred_element_type=F32) + bias
            y_ref[rows, cols] = (u_ref[rows, cols].astype(F32) * sv).astype(y_ref.dtype)


def _gmlp_gate(z, ln_g, ln_b, w_s, b_s):
    t, w2 = z.shape
    width = w2 // 2
    n_groups, chunk, _ = w_s.shape
    tm = _tile(t, 4 * chunk)
    return pl.pallas_call(
        _gmlp_gate_kernel,
        grid=(t // tm,),
        in_specs=[
            pl.BlockSpec((tm, width), lambda i: (i, 0)),
            pl.BlockSpec((tm, width), lambda i: (i, 1)),
            pl.BlockSpec((1, width), lambda i: (0, 0)),
            pl.BlockSpec((1, width), lambda i: (0, 0)),
            pl.BlockSpec((n_groups, chunk, chunk), lambda i: (0, 0, 0)),
            pl.BlockSpec((chunk, n_groups), lambda i: (0, 0)),
        ],
        out_specs=pl.BlockSpec((tm, width), lambda i: (i, 0)),
        out_shape=jax.ShapeDtypeStruct((t, width), BF16),
        scratch_shapes=[pltpu.VMEM((tm, width), BF16)],
        compiler_params=_params(("parallel",)),
        name="gmlp_gate",
    )(z, z, ln_g.reshape(1, width), ln_b.reshape(1, width), w_s, b_s.T)


def _diff_attn_kernel(qi_ref, kj_ref, lam_ref, q_ref, k_ref, v_ref, g_ref, o_ref,
                      qs_ref, m_ref, l_ref, acc_ref, *, out_scale):
    p_idx = pl.program_id(2)
    qi, kj = qi_ref[p_idx], kj_ref[p_idx]
    tq, dh2 = q_ref.shape
    tk = k_ref.shape[0]
    half = dh2 // 2

    @pl.when(kj == 0)
    def _():
        q = q_ref[...]
        lane = lax.broadcasted_iota(I32, q.shape, 1)
        qs_ref[:tq, :] = jnp.where(lane < half, q, jnp.zeros_like(q))
        qs_ref[tq:, :] = jnp.where(lane >= half, q, jnp.zeros_like(q))
        m_ref[...] = jnp.full_like(m_ref, -jnp.inf)
        l_ref[...] = jnp.zeros_like(l_ref)
        acc_ref[...] = jnp.zeros_like(acc_ref)

    s = lax.dot_general(qs_ref[...], k_ref[...], (((1,), (1,)), ((), ())),
                        preferred_element_type=F32)

    def update(s):
        m_old = m_ref[...]
        m_new = jnp.maximum(m_old, jnp.max(s, axis=-1, keepdims=True))
        p = jnp.exp(s - m_new)
        alpha = jnp.exp(m_old - m_new)
        l_ref[...] = alpha * l_ref[...] + jnp.sum(p, axis=-1, keepdims=True)
        acc_ref[...] = alpha * acc_ref[...] + jnp.dot(p.astype(BF16), v_ref[...],
                                                      preferred_element_type=F32)
        m_ref[...] = m_new

    @pl.when(kj < qi)
    def _():
        update(s)

    @pl.when(kj == qi)
    def _():
        row = lax.broadcasted_iota(I32, s.shape, 0) % tq
        col = lax.broadcasted_iota(I32, s.shape, 1)
        update(jnp.where(col <= row, s, -jnp.inf))
        o = acc_ref[...] / l_ref[...]
        o = o[:tq] - lam_ref[0] * o[tq:]
        o = o * _rms_scale(o) * g_ref[...] * out_scale
        o_ref[...] = o.astype(o_ref.dtype)


def _diff_attention(qkv, lam, subln_g, *, batch, seq, n_heads, out_scale):
    t = qkv.shape[0]
    dh2 = qkv.shape[1] // (3 * n_heads)
    tq = _tile(seq, 512)
    nq = seq // tq
    pairs = [(i, j) for i in range(nq) for j in range(i + 1)]
    qi_list = jnp.asarray([p[0] for p in pairs], I32)
    kj_list = jnp.asarray([p[1] for p in pairs], I32)
    grid_spec = pltpu.PrefetchScalarGridSpec(
        num_scalar_prefetch=2,
        grid=(batch, n_heads, len(pairs)),
        in_specs=[
            pl.BlockSpec(memory_space=pltpu.SMEM),
            pl.BlockSpec((tq, dh2), lambda b, h, p, qi, kj: (b * nq + qi[p], h)),
            pl.BlockSpec((tq, dh2), lambda b, h, p, qi, kj: (b * nq + kj[p], n_heads + h)),
            pl.BlockSpec((tq, dh2), lambda b, h, p, qi, kj: (b * nq + kj[p], 2 * n_heads + h)),
            pl.BlockSpec((1, dh2), lambda b, h, p, qi, kj: (0, 0)),
        ],
        out_specs=pl.BlockSpec((tq, dh2), lambda b, h, p, qi, kj: (b * nq + qi[p], h)),
        scratch_shapes=[
            pltpu.VMEM((2 * tq, dh2), BF16),
            pltpu.VMEM((2 * tq, 1), F32),
            pltpu.VMEM((2 * tq, 1), F32),
            pltpu.VMEM((2 * tq, dh2), F32),
        ],
    )
    return pl.pallas_call(
        functools.partial(_diff_attn_kernel, out_scale=out_scale),
        grid_spec=grid_spec,
        out_shape=jax.ShapeDtypeStruct((t, n_heads * dh2), BF16),
        compiler_params=_params(("parallel", "parallel", "arbitrary")),
        name="diff_attention",
    )(qi_list, kj_list, lam.reshape(1), qkv, qkv, qkv, subln_g.reshape(1, dh2))


def _router_kernel(x_ref, g_ref, rw_ref, rb_ref, hp_ref, r_ref, cnt_ref, carry_ref):
    @pl.when(pl.program_id(0) == 0)
    def _():
        carry_ref[...] = jnp.zeros_like(carry_ref)

    x = x_ref[...]
    tm, d = x.shape
    n_exp = rw_ref.shape[1]
    h = x * _rms_scale(x) * g_ref[...]
    h_hi = h.astype(BF16)
    h_hi32 = h_hi.astype(F32)
    half = d // 2
    lo_word = pltpu.bitcast(h_hi32[:, :half], U32) >> 16
    hi_word = pltpu.bitcast(h_hi32[:, half:], U32) & jnp.uint32(HI16)
    hp_ref[...] = lo_word | hi_word

    h_lo = (h - h_hi32).astype(BF16)
    rw = rw_ref[...]
    w_hi = rw.astype(BF16)
    w_lo = (rw - w_hi.astype(F32)).astype(BF16)
    logits = (jnp.dot(h_hi, w_hi, preferred_element_type=F32)
              + jnp.dot(h_lo, w_hi, preferred_element_type=F32)
              + jnp.dot(h_hi, w_lo, preferred_element_type=F32)) + rb_ref[...]

    e_iota = lax.broadcasted_iota(I32, (tm, n_exp), 1).astype(F32)
    work = logits
    vals, onehots, idxs = [], [], []
    for _ in range(TOP_K):
        m = jnp.max(work, axis=-1, keepdims=True)
        idx = jnp.min(jnp.where(work == m, e_iota, float(n_exp)), axis=-1, keepdims=True)
        hit = e_iota == idx
        vals.append(m)
        idxs.append(idx)
        onehots.append(hit)
        work = jnp.where(hit, -jnp.inf, work)
    exps = [jnp.exp(v - vals[0]) for v in vals]
    denom = functools.reduce(lambda a, b: a + b, exps)

    chosen = functools.reduce(jnp.logical_or, onehots)
    chosen_f = jnp.where(chosen, 1.0, 0.0)
    earlier = (lax.broadcasted_iota(I32, (tm, tm), 0) > lax.broadcasted_iota(I32, (tm, tm), 1))
    before = jnp.dot(jnp.where(earlier, 1.0, 0.0).astype(BF16), chosen_f.astype(BF16),
                     preferred_element_type=F32) + carry_ref[...]
    carry_ref[...] = carry_ref[...] + jnp.sum(chosen_f, axis=0, keepdims=True)
    cnt_ref[...] = carry_ref[...]

    lane = lax.broadcasted_iota(I32, r_ref.shape, 1)
    out = jnp.zeros(r_ref.shape, F32)
    for k in range(TOP_K):
        rank = jnp.sum(jnp.where(onehots[k], before, 0.0), axis=-1, keepdims=True)
        out = jnp.where(lane == k, idxs[k], out)
        out = jnp.where(lane == TOP_K + k, rank, out)
        out = jnp.where(lane == 2 * TOP_K + k, exps[k] / denom, out)
    r_ref[...] = out


def _router(x, g, router_w, router_b):
    t, d = x.shape
    n_exp = router_w.shape[1]
    tm = _tile(t, 512)
    return pl.pallas_call(
        _router_kernel,
        grid=(t // tm,),
        in_specs=[
            pl.BlockSpec((tm, d), lambda i: (i, 0)),
            pl.BlockSpec((1, d), lambda i: (0, 0)),
            pl.BlockSpec((d, n_exp), lambda i: (0, 0)),
            pl.BlockSpec((1, n_exp), lambda i: (0, 0)),
        ],
        out_specs=[
            pl.BlockSpec((tm, d // 2), lambda i: (i, 0)),
            pl.BlockSpec((tm, V7X_LANES), lambda i: (i, 0)),
            pl.BlockSpec((1, n_exp), lambda i: (0, 0)),
        ],
        out_shape=[
            jax.ShapeDtypeStruct((t, d // 2), U32),
            jax.ShapeDtypeStruct((t, V7X_LANES), F32),
            jax.ShapeDtypeStruct((1, n_exp), F32),
        ],
        scratch_shapes=[pltpu.VMEM((1, n_exp), F32)],
        compiler_params=_params(("arbitrary",)),
        name="moe_router",
    )(x, g.reshape(1, d), router_w, router_b.reshape(1, n_exp))


def _dispatch_kernel(dest_ref, hp_ref, xs_init_ref, xs_ref, sem):
    del xs_init_ref
    tm = hp_ref.shape[0]

    def row_copy(r, k):
        return pltpu.make_async_copy(hp_ref.at[pl.ds(r, 1), :],
                                     xs_ref.at[pl.ds(dest_ref[r * TOP_K + k], 1), :], sem)

    def issue(r, carry):
        for k in range(TOP_K):
            row_copy(r, k).start()
        return carry

    def drain(r, carry):
        for k in range(TOP_K):
            row_copy(r, k).wait()
        return carry

    lax.fori_loop(0, tm, issue, 0)
    lax.fori_loop(0, tm, drain, 0)


def _dispatch(hp, dest_flat, n_slots):
    t, dw = hp.shape
    tm = _tile(t, 256)
    xs_init = jnp.zeros((n_slots, dw), U32)
    return pl.pallas_call(
        _dispatch_kernel,
        grid=(t // tm,),
        in_specs=[
            pl.BlockSpec((tm * TOP_K,), lambda i: (i,), memory_space=pltpu.SMEM),
            pl.BlockSpec((tm, dw), lambda i: (i, 0)),
            pl.BlockSpec(memory_space=pl.ANY),
        ],
        out_specs=pl.BlockSpec(memory_space=pl.ANY),
        out_shape=jax.ShapeDtypeStruct((n_slots, dw), U32),
        scratch_shapes=[pltpu.SemaphoreType.DMA(())],
        input_output_aliases={2: 0},
        compiler_params=_params(("arbitrary",)),
        name="moe_dispatch",
    )(dest_flat, hp, xs_init)


def _experts_kernel(te_ref, nu_ref, xs_ref, wg_ref, wu_ref, wd_ref, bg_ref, bu_ref, bd_ref, y_ref, x_ref):
    i, c = pl.program_id(0), pl.program_id(1)

    @pl.when(i < nu_ref[0])
    def _():
        half = xs_ref.shape[1]

        @pl.when(c == 0)
        def _():
            words = xs_ref[...]
            x_ref[:, :half] = pltpu.bitcast(words << 16, F32).astype(BF16)
            x_ref[:, half:] = pltpu.bitcast(words & jnp.uint32(HI16), F32).astype(BF16)

        x = x_ref[...]
        gate = jnp.dot(x, wg_ref[...], preferred_element_type=F32) + bg_ref[...]
        up = jnp.dot(x, wu_ref[...], preferred_element_type=F32) + bu_ref[...]
        gate = jnp.minimum(gate, SWIGLU_LIMIT)
        up = jnp.clip(up, -SWIGLU_LIMIT, SWIGLU_LIMIT)
        act = gate * jax.nn.sigmoid(SWIGLU_ALPHA * gate) * (up + 1.0)
        part = jnp.dot(act.astype(BF16), wd_ref[...], preferred_element_type=F32)

        @pl.when(c == 0)
        def _():
            y_ref[...] = part + bd_ref[...]

        @pl.when(c > 0)
        def _():
            y_ref[...] += part

    @pl.when(jnp.logical_and(i >= nu_ref[0], c == 0))
    def _():
        y_ref[...] = jnp.zeros_like(y_ref)


def _experts(xs, tile_expert, n_used, w_g, w_u, w_d, b_g, b_u, b_d, *, bm):
    n_slots, dw = xs.shape
    n_exp, d, f = w_g.shape
    fc = _tile(f, 512)
    nt, nc = n_slots // bm, f // fc

    def tile_idx(i, nu):
        return jnp.minimum(i, nu[0] - 1)

    def chunk_idx(i, c, nu):
        return jnp.where(i < nu[0], c, nc - 1)

    grid_spec = pltpu.PrefetchScalarGridSpec(
        num_scalar_prefetch=2,
        grid=(nt, nc),
        in_specs=[
            pl.BlockSpec((bm, dw), lambda i, c, te, nu: (tile_idx(i, nu), 0)),
            pl.BlockSpec((None, d, fc), lambda i, c, te, nu: (te[tile_idx(i, nu)], 0, chunk_idx(i, c, nu))),
            pl.BlockSpec((None, d, fc), lambda i, c, te, nu: (te[tile_idx(i, nu)], 0, chunk_idx(i, c, nu))),
            pl.BlockSpec((None, fc, d), lambda i, c, te, nu: (te[tile_idx(i, nu)], chunk_idx(i, c, nu), 0)),
            pl.BlockSpec((None, 1, fc), lambda i, c, te, nu: (te[tile_idx(i, nu)], 0, chunk_idx(i, c, nu))),
            pl.BlockSpec((None, 1, fc), lambda i, c, te, nu: (te[tile_idx(i, nu)], 0, chunk_idx(i, c, nu))),
            pl.BlockSpec((None, 1, d), lambda i, c, te, nu: (te[tile_idx(i, nu)], 0, 0)),
        ],
        out_specs=pl.BlockSpec((bm, d), lambda i, c, te, nu: (i, 0)),
        scratch_shapes=[pltpu.VMEM((bm, d), BF16)],
    )
    return pl.pallas_call(
        _experts_kernel,
        grid_spec=grid_spec,
        out_shape=jax.ShapeDtypeStruct((n_slots, d), F32),
        compiler_params=_params(("arbitrary", "arbitrary")),
        name="moe_experts",
    )(tile_expert, n_used, xs, w_g, w_u, w_d, b_g, b_u, b_d)


def _combine_kernel(dest_ref, x_ref, r_ref, y_ref, *rest, final_norm):
    if final_norm:
        g_ref, o_ref, buf_ref, sem = rest
    else:
        o_ref, buf_ref, sem = rest
    tm = x_ref.shape[0]

    def row_copy(r, k):
        return pltpu.make_async_copy(y_ref.at[pl.ds(dest_ref[r * TOP_K + k], 1), :],
                                     buf_ref.at[k, pl.ds(r, 1), :], sem)

    def issue(r, carry):
        for k in range(TOP_K):
            row_copy(r, k).start()
        return carry

    def drain(r, carry):
        for k in range(TOP_K):
            row_copy(r, k).wait()
        return carry

    lax.fori_loop(0, tm, issue, 0)
    lax.fori_loop(0, tm, drain, 0)

    acc = x_ref[...]
    for k in range(TOP_K):
        acc = acc + r_ref[:, 2 * TOP_K + k:2 * TOP_K + k + 1] * buf_ref[k]
    if final_norm:
        acc = acc * _rms_scale(acc) * g_ref[...]
    o_ref[...] = acc


def _combine(x, r, y, dest_flat, final_g=None):
    t, d = x.shape
    tm = _tile(t, 256)
    in_specs = [
        pl.BlockSpec((tm * TOP_K,), lambda i: (i,), memory_space=pltpu.SMEM),
        pl.BlockSpec((tm, d), lambda i: (i, 0)),
        pl.BlockSpec((tm, V7X_LANES), lambda i: (i, 0)),
        pl.BlockSpec(memory_space=pl.ANY),
    ]
    args = [dest_flat, x, r, y]
    if final_g is not None:
        in_specs.append(pl.BlockSpec((1, d), lambda i: (0, 0)))
        args.append(final_g.reshape(1, d))
    return pl.pallas_call(
        functools.partial(_combine_kernel, final_norm=final_g is not None),
        grid=(t // tm,),
        in_specs=in_specs,
        out_specs=pl.BlockSpec((tm, d), lambda i: (i, 0)),
        out_shape=jax.ShapeDtypeStruct((t, d), F32),
        scratch_shapes=[pltpu.VMEM((TOP_K, tm, d), F32), pltpu.SemaphoreType.DMA(())],
        compiler_params=_params(("arbitrary",)),
        name="moe_combine",
    )(*args)


def _moe(x, norm_g, router_w, router_b, w_gate_up, b_gate_up, w_down, b_down, final_g=None):
    t, d = x.shape
    n_exp = router_w.shape[1]
    bm = _tile(t, 512)
    n_tiles = t * TOP_K // bm + n_exp
    hp, r, cnt = _router(x, norm_g, router_w, router_b)

    top_i = r[:, :TOP_K].astype(I32)
    rank = r[:, TOP_K:2 * TOP_K].astype(I32)
    counts = cnt[0].astype(I32)
    padded = (counts + bm - 1) // bm * bm
    pad_end = jnp.cumsum(padded)
    pad_start = pad_end - padded
    dest_flat = (pad_start[top_i] + rank).reshape(-1)
    n_used = (pad_end[-1:] // bm).astype(I32)
    tile_expert = jnp.clip(jnp.searchsorted(pad_end, jnp.arange(n_tiles, dtype=I32) * bm, side='right'),
                           0, n_exp - 1).astype(I32)

    xs = _dispatch(hp, dest_flat, n_tiles * bm)
    f = w_down.shape[1]
    y = _experts(xs, tile_expert, n_used,
                 w_gate_up[:, :, 0::2].astype(BF16), w_gate_up[:, :, 1::2].astype(BF16), w_down.astype(BF16),
                 b_gate_up[:, 0::2].reshape(n_exp, 1, f), b_gate_up[:, 1::2].reshape(n_exp, 1, f),
                 b_down.reshape(n_exp, 1, d), bm=bm)
    return _combine(x, r, y, dest_flat, final_g)


def _rope_tables(seq, dh, rot_dim, q_scale):
    half = rot_dim // 2
    inv_freq = ROPE_THETA ** (-jnp.arange(0, rot_dim, 2, dtype=F32) / rot_dim)
    ang = jnp.arange(seq, dtype=F32)[:, None] * inv_freq[None, :]
    cos, sin = jnp.cos(ang), jnp.sin(ang)
    ones = jnp.ones((seq, dh - rot_dim), F32)
    zeros_rest = jnp.zeros((seq, dh - rot_dim), F32)
    zeros_half = jnp.zeros((seq, half), F32)
    c = jnp.concatenate([cos, cos, ones], axis=1)
    s1 = jnp.concatenate([zeros_half, sin, zeros_rest], axis=1)
    s2 = jnp.concatenate([-sin, zeros_half, zeros_rest], axis=1)
    k_tab = jnp.stack([jnp.tile(a, (1, 2)) for a in (c, s1, s2)])
    return jnp.stack([k_tab * q_scale, k_tab])


def _gmlp_layer(x, norm_g, w_in, ln_g, ln_b, w_s, b_s, w_out):
    z = _norm_matmul(x, norm_g, w_in.astype(BF16))
    y = _gmlp_gate(z, ln_g, ln_b, w_s, b_s)
    return _matmul_residual(y, w_out.astype(BF16), x)


def _attn_layer(x, norm_g, w_qkv, lq1, lk1, lq2, lk2, subln_g, w_o, lambda_init, *, batch, seq):
    d = x.shape[1]
    dh = lq1.shape[0]
    n_heads = d // (2 * dh)
    rot_dim = dh // 4
    tab = _rope_tables(seq, dh, rot_dim, dh ** -0.5)
    qkv = _norm_matmul(x, norm_g, w_qkv.astype(BF16), rope_tab=tab, seq=seq,
                       n_rope_cols=2 * d, rot_half=rot_dim // 2)
    lam = jnp.exp(jnp.sum(lq1 * lk1)) - jnp.exp(jnp.sum(lq2 * lk2)) + lambda_init
    o = _diff_attention(qkv, lam.astype(F32), subln_g, batch=batch, seq=seq, n_heads=n_heads,
                        out_scale=1.0 - lambda_init)
    return _matmul_residual(o, w_o.astype(BF16), x)


def kernel(x, attn_norm_g, ffn_norm_g, final_norm_g, a_w_in, a_ln_g, a_ln_b, a_w_s, a_b_s, a_w_out,
           b_w_qkv, b_lambda_q1, b_lambda_k1, b_lambda_q2, b_lambda_k2, b_subln_g, b_w_o,
           router_w, router_b, w_gate_up, b_gate_up, w_down, b_down):
    batch, seq, d = x.shape
    depth = attn_norm_g.shape[0]
    n_mixers = 2
    h = x.reshape(batch * seq, d)
    for i in range(depth):
        j = i // n_mixers
        if i % n_mixers == 0:
            h = _gmlp_layer(h, attn_norm_g[i], a_w_in[j], a_ln_g[j], a_ln_b[j], a_w_s[j], a_b_s[j], a_w_out[j])
        else:
            lambda_init = 0.8 - 0.6 * math.exp(-0.3 * i)
            h = _attn_layer(h, attn_norm_g[i], b_w_qkv[j], b_lambda_q1[j], b_lambda_k1[j], b_lambda_q2[j],
                            b_lambda_k2[j], b_subln_g[j], b_w_o[j], lambda_init, batch=batch, seq=seq)
        h = _moe(h, ffn_norm_g[i], router_w[i], router_b[i], w_gate_up[i], b_gate_up[i], w_down[i], b_down[i],
                 final_g=final_norm_g if i == depth - 1 else None)
    return h.reshape(batch, seq, d)
```

```python
import functools
import math

import jax
import jax.numpy as jnp
from jax import lax
from jax.experimental import pallas as pl
from jax.experimental.pallas import tpu as pltpu

F32 = jnp.float32
BF16 = jnp.bfloat16
I32 = jnp.int32
U32 = jnp.uint32

EPS = 1e-5
TOP_K = 4
ROPE_THETA = 500000.0
SWIGLU_LIMIT = 7.0
SWIGLU_ALPHA = 1.702

V7X_LANES = 128
V7X_VMEM_BYTES = 64 * 1024 * 1024
VMEM_LIMIT_BYTES = V7X_VMEM_BYTES * 7 // 8
HI16 = 0xFFFF0000

ATTN_ROW_CHUNK = 512
MOE_ROW_TILE = 1024
MOE_SUB_ROWS = 256
MOE_GU_CHUNK = 512


def _tile(dim, want):
    t = min(dim, want)
    while dim % t:
        t //= 2
    return t


def _params(semantics):
    return pltpu.CompilerParams(dimension_semantics=semantics, vmem_limit_bytes=VMEM_LIMIT_BYTES)


def _rms_scale(x):
    return lax.rsqrt(jnp.mean(x * x, axis=-1, keepdims=True) + EPS)


def _norm_matmul_gelu_kernel(x_ref, g_ref, w_ref, o_ref, xn_ref):
    @pl.when(pl.program_id(1) == 0)
    def _():
        x = x_ref[...]
        xn_ref[...] = (x * _rms_scale(x) * g_ref[...]).astype(BF16)

    y = jnp.dot(xn_ref[...], w_ref[...], preferred_element_type=F32)
    o_ref[...] = jax.nn.gelu(y).astype(o_ref.dtype)


def _norm_matmul_rope_kernel(x_ref, g_ref, w_ref, tab_ref, o_ref, xn_ref, *, n_rope_tiles, shift):
    j = pl.program_id(1)

    @pl.when(j == 0)
    def _():
        x = x_ref[...]
        xn_ref[...] = (x * _rms_scale(x) * g_ref[...]).astype(BF16)

    y = jnp.dot(xn_ref[...], w_ref[...], preferred_element_type=F32)

    @pl.when(j < n_rope_tiles)
    def _():
        c, s1, s2 = tab_ref[0], tab_ref[1], tab_ref[2]
        for h in range(y.shape[1] // V7X_LANES):
            cols = slice(h * V7X_LANES, (h + 1) * V7X_LANES)
            yh = y[:, cols]
            r = (yh * c + pltpu.roll(yh, shift, 1) * s1
                 + pltpu.roll(yh, V7X_LANES - shift, 1) * s2)
            o_ref[:, cols] = r.astype(o_ref.dtype)

    @pl.when(j >= n_rope_tiles)
    def _():
        o_ref[...] = y.astype(o_ref.dtype)


def _norm_matmul(x, g, w, *, rope_tab=None, seq=None, n_rope_cols=0, rot_half=0):
    t, k = x.shape
    n = w.shape[1]
    tm = _tile(t if seq is None else seq, 1024)
    tn = _tile(n if rope_tab is None else n_rope_cols // 2, 1024)
    grid = (t // tm, n // tn)
    in_specs = [
        pl.BlockSpec((tm, k), lambda i, j: (i, 0)),
        pl.BlockSpec((1, k), lambda i, j: (0, 0)),
        pl.BlockSpec((k, tn), lambda i, j: (0, j)),
    ]
    args = [x, g.reshape(1, k), w]
    if rope_tab is None:
        body = _norm_matmul_gelu_kernel
    else:
        assert n_rope_cols % (2 * tn) == 0 and tn % V7X_LANES == 0
        n_rope_tiles = n_rope_cols // tn
        s_tiles = seq // tm
        in_specs.append(pl.BlockSpec(
            (None, 3, tm, V7X_LANES),
            lambda i, j: (jnp.minimum(j // (n_rope_tiles // 2), 1), 0, i % s_tiles, 0)))
        args.append(rope_tab)
        body = functools.partial(_norm_matmul_rope_kernel, n_rope_tiles=n_rope_tiles, shift=rot_half)
    return pl.pallas_call(
        body,
        grid=grid,
        in_specs=in_specs,
        out_specs=pl.BlockSpec((tm, tn), lambda i, j: (i, j)),
        out_shape=jax.ShapeDtypeStruct((t, n), BF16),
        scratch_shapes=[pltpu.VMEM((tm, k), BF16)],
        compiler_params=_params(("parallel", "arbitrary")),
        name="norm_matmul_gelu" if rope_tab is None else "norm_matmul_rope",
    )(*args)


def _matmul_residual_kernel(a_ref, w_ref, r_ref, o_ref):
    o_ref[...] = r_ref[...] + jnp.dot(a_ref[...], w_ref[...], preferred_element_type=F32)


def _matmul_residual(a, w, res):
    t, k = a.shape
    n = w.shape[1]
    tm, tn = _tile(t, 1024), _tile(n, 1024)
    return pl.pallas_call(
        _matmul_residual_kernel,
        grid=(t // tm, n // tn),
        in_specs=[
            pl.BlockSpec((tm, k), lambda i, j: (i, 0)),
            pl.BlockSpec((k, tn), lambda i, j: (0, j)),
            pl.BlockSpec((tm, tn), lambda i, j: (i, j)),
        ],
        out_specs=pl.BlockSpec((tm, tn), lambda i, j: (i, j)),
        out_shape=jax.ShapeDtypeStruct((t, n), F32),
        compiler_params=_params(("parallel", "parallel")),
        name="matmul_residual",
    )(a, w, res)


def _gmlp_gate_kernel(u_ref, v_ref, lng_ref, lnb_ref, ws_ref, bst_ref, y_ref, vn_ref):
    n_groups, chunk, _ = ws_ref.shape
    tm, width = u_ref.shape
    gd = width // n_groups
    v = v_ref[...].astype(F32)
    vc = v - jnp.mean(v, axis=-1, keepdims=True)
    inv = lax.rsqrt(jnp.mean(vc * vc, axis=-1, keepdims=True) + EPS)
    vn_ref[...] = (vc * inv * lng_ref[...] + lnb_ref[...]).astype(BF16)
    causal = (lax.broadcasted_iota(I32, (chunk, chunk), 0) >= lax.broadcasted_iota(I32, (chunk, chunk), 1))
    for g in range(n_groups):
        w = jnp.where(causal, ws_ref[g], 0.0).astype(BF16)
        bias = bst_ref[:, g:g + 1]
        cols = slice(g * gd, (g + 1) * gd)
        for ci in range(tm // chunk):
            rows = slice(ci * chunk, (ci + 1) * chunk)
            sv = jnp.dot(w, vn_ref[rows, cols], preferred_element_type=F32) + bias
            y_ref[rows, cols] = (u_ref[rows, cols].astype(F32) * sv).astype(y_ref.dtype)


def _gmlp_gate(z, ln_g, ln_b, w_s, b_s):
    t, w2 = z.shape
    width = w2 // 2
    n_groups, chunk, _ = w_s.shape
    tm = _tile(t, 4 * chunk)
    return pl.pallas_call(
        _gmlp_gate_kernel,
        grid=(t // tm,),
        in_specs=[
            pl.BlockSpec((tm, width), lambda i: (i, 0)),
            pl.BlockSpec((tm, width), lambda i: (i, 1)),
            pl.BlockSpec((1, width), lambda i: (0, 0)),
            pl.BlockSpec((1, width), lambda i: (0, 0)),
            pl.BlockSpec((n_groups, chunk, chunk), lambda i: (0, 0, 0)),
            pl.BlockSpec((chunk, n_groups), lambda i: (0, 0)),
        ],
        out_specs=pl.BlockSpec((tm, width), lambda i: (i, 0)),
        out_shape=jax.ShapeDtypeStruct((t, width), BF16),
        scratch_shapes=[pltpu.VMEM((tm, width), BF16)],
        compiler_params=_params(("parallel",)),
        name="gmlp_gate",
    )(z, z, ln_g.reshape(1, width), ln_b.reshape(1, width), w_s, b_s.T)


def _diff_attn_kernel(lam_ref, q_ref, k_ref, v_ref, g_ref, o_ref, qs_ref, s_ref, m_ref, l_ref, acc_ref, *,
                      out_scale):
    qi = pl.program_id(2)
    tq, dh2 = q_ref.shape
    half = dh2 // 2
    row_chunk = _tile(tq, ATTN_ROW_CHUNK)

    q = q_ref[...]
    lane = lax.broadcasted_iota(I32, q.shape, 1)
    qs_ref[:tq, :] = jnp.where(lane < half, q, jnp.zeros_like(q))
    qs_ref[tq:, :] = jnp.where(lane >= half, q, jnp.zeros_like(q))
    m_ref[...] = jnp.full_like(m_ref, -jnp.inf)
    l_ref[...] = jnp.zeros_like(l_ref)
    acc_ref[...] = jnp.zeros_like(acc_ref)

    def kv_rows(kj):
        return pl.ds(pl.multiple_of(kj * tq, tq), tq)

    def scores(kj, slot):
        s_ref[slot] = lax.dot_general(qs_ref[...], k_ref[kv_rows(kj), :], (((1,), (1,)), ((), ())),
                                      preferred_element_type=F32)

    def softmax_pv(kj, slot, masked):
        v = v_ref[kv_rows(kj), :]
        for r in range(2 * tq // row_chunk):
            rs = slice(r * row_chunk, (r + 1) * row_chunk)
            s = s_ref[slot, rs, :]
            if masked:
                row = lax.broadcasted_iota(I32, s.shape, 0) + (r * row_chunk) % tq
                col = lax.broadcasted_iota(I32, s.shape, 1)
                s = jnp.where(col <= row, s, -jnp.inf)
            m_old = m_ref[rs, :]
            m_new = jnp.maximum(m_old, jnp.max(s, axis=-1, keepdims=True))
            p = jnp.exp(s - m_new)
            alpha = jnp.exp(m_old - m_new)
            l_ref[rs, :] = alpha * l_ref[rs, :] + jnp.sum(p, axis=-1, keepdims=True)
            acc_ref[rs, :] = alpha * acc_ref[rs, :] + jnp.dot(p.astype(BF16), v, preferred_element_type=F32)
            m_ref[rs, :] = m_new

    scores(0, 0)

    def off_diagonal(kj, carry):
        slot = kj % 2
        softmax_pv(kj, slot, masked=False)
        scores(kj + 1, 1 - slot)
        return carry

    lax.fori_loop(0, qi, off_diagonal, 0)
    softmax_pv(qi, qi % 2, masked=True)

    o = acc_ref[...] / l_ref[...]
    o = o[:tq] - lam_ref[0] * o[tq:]
    o = o * _rms_scale(o) * g_ref[...] * out_scale
    o_ref[...] = o.astype(o_ref.dtype)


def _diff_attention(qkv, lam, subln_g, *, batch, seq, n_heads, out_scale):
    t = qkv.shape[0]
    dh2 = qkv.shape[1] // (3 * n_heads)
    tq = _tile(seq, 512)
    nq = seq // tq
    return pl.pallas_call(
        functools.partial(_diff_attn_kernel, out_scale=out_scale),
        grid=(batch, n_heads, nq),
        in_specs=[
            pl.BlockSpec(memory_space=pltpu.SMEM),
            pl.BlockSpec((tq, dh2), lambda b, h, i: (b * nq + i, h)),
            pl.BlockSpec((seq, dh2), lambda b, h, i: (b, n_heads + h)),
            pl.BlockSpec((seq, dh2), lambda b, h, i: (b, 2 * n_heads + h)),
            pl.BlockSpec((1, dh2), lambda b, h, i: (0, 0)),
        ],
        out_specs=pl.BlockSpec((tq, dh2), lambda b, h, i: (b * nq + i, h)),
        out_shape=jax.ShapeDtypeStruct((t, n_heads * dh2), BF16),
        scratch_shapes=[
            pltpu.VMEM((2 * tq, dh2), BF16),
            pltpu.VMEM((2, 2 * tq, tq), F32),
            pltpu.VMEM((2 * tq, 1), F32),
            pltpu.VMEM((2 * tq, 1), F32),
            pltpu.VMEM((2 * tq, dh2), F32),
        ],
        compiler_params=_params(("parallel", "parallel", "arbitrary")),
        name="diff_attention",
    )(lam.reshape(1), qkv, qkv, qkv, subln_g.reshape(1, dh2))


def _router_kernel(x_ref, g_ref, rw_ref, rb_ref, hp_ref, r_ref, cnt_ref, carry_ref):
    @pl.when(pl.program_id(0) == 0)
    def _():
        carry_ref[...] = jnp.zeros_like(carry_ref)

    x = x_ref[...]
    tm, d = x.shape
    n_exp = rw_ref.shape[1]
    h = x * _rms_scale(x) * g_ref[...]
    h_hi = h.astype(BF16)
    h_hi32 = h_hi.astype(F32)
    half = d // 2
    lo_word = pltpu.bitcast(h_hi32[:, :half], U32) >> 16
    hi_word = pltpu.bitcast(h_hi32[:, half:], U32) & jnp.uint32(HI16)
    hp_ref[...] = lo_word | hi_word

    h_lo = (h - h_hi32).astype(BF16)
    rw = rw_ref[...]
    w_hi = rw.astype(BF16)
    w_lo = (rw - w_hi.astype(F32)).astype(BF16)
    logits = (jnp.dot(h_hi, w_hi, preferred_element_type=F32)
              + jnp.dot(h_lo, w_hi, preferred_element_type=F32)
              + jnp.dot(h_hi, w_lo, preferred_element_type=F32)) + rb_ref[...]

    e_iota = lax.broadcasted_iota(I32, (tm, n_exp), 1).astype(F32)
    work = logits
    vals, onehots, idxs = [], [], []
    for _ in range(TOP_K):
        m = jnp.max(work, axis=-1, keepdims=True)
        idx = jnp.min(jnp.where(work == m, e_iota, float(n_exp)), axis=-1, keepdims=True)
        hit = e_iota == idx
        vals.append(m)
        idxs.append(idx)
        onehots.append(hit)
        work = jnp.where(hit, -jnp.inf, work)
    exps = [jnp.exp(v - vals[0]) for v in vals]
    denom = functools.reduce(lambda a, b: a + b, exps)

    chosen = functools.reduce(jnp.logical_or, onehots)
    chosen_f = jnp.where(chosen, 1.0, 0.0)
    earlier = (lax.broadcasted_iota(I32, (tm, tm), 0) > lax.broadcasted_iota(I32, (tm, tm), 1))
    before = jnp.dot(jnp.where(earlier, 1.0, 0.0).astype(BF16), chosen_f.astype(BF16),
                     preferred_element_type=F32) + carry_ref[...]
    carry_ref[...] = carry_ref[...] + jnp.sum(chosen_f, axis=0, keepdims=True)
    cnt_ref[...] = carry_ref[...]

    lane = lax.broadcasted_iota(I32, r_ref.shape, 1)
    out = jnp.zeros(r_ref.shape, F32)
    for k in range(TOP_K):
        rank = jnp.sum(jnp.where(onehots[k], before, 0.0), axis=-1, keepdims=True)
        out = jnp.where(lane == k, idxs[k], out)
        out = jnp.where(lane == TOP_K + k, rank, out)
        out = jnp.where(lane == 2 * TOP_K + k, exps[k] / denom, out)
    r_ref[...] = out


def _router(x, g, router_w, router_b):
    t, d = x.shape
    n_exp = router_w.shape[1]
    tm = _tile(t, 512)
    return pl.pallas_call(
        _router_kernel,
        grid=(t // tm,),
        in_specs=[
            pl.BlockSpec((tm, d), lambda i: (i, 0)),
            pl.BlockSpec((1, d), lambda i: (0, 0)),
            pl.BlockSpec((d, n_exp), lambda i: (0, 0)),
            pl.BlockSpec((1, n_exp), lambda i: (0, 0)),
        ],
        out_specs=[
            pl.BlockSpec((tm, d // 2), lambda i: (i, 0)),
            pl.BlockSpec((tm, V7X_LANES), lambda i: (i, 0)),
            pl.BlockSpec((1, n_exp), lambda i: (0, 0)),
        ],
        out_shape=[
            jax.ShapeDtypeStruct((t, d // 2), U32),
            jax.ShapeDtypeStruct((t, V7X_LANES), F32),
            jax.ShapeDtypeStruct((1, n_exp), F32),
        ],
        scratch_shapes=[pltpu.VMEM((1, n_exp), F32)],
        compiler_params=_params(("arbitrary",)),
        name="moe_router",
    )(x, g.reshape(1, d), router_w, router_b.reshape(1, n_exp))


def _dispatch_kernel(dest_ref, hp_ref, xs_init_ref, xs_ref, sem):
    del xs_init_ref
    tm = hp_ref.shape[0]

    def row_copy(r, k):
        return pltpu.make_async_copy(hp_ref.at[pl.ds(r, 1), :],
                                     xs_ref.at[pl.ds(dest_ref[r * TOP_K + k], 1), :], sem)

    def issue(r, carry):
        for k in range(TOP_K):
            row_copy(r, k).start(priority=k % 2)
        return carry

    def drain(r, carry):
        for k in range(TOP_K):
            row_copy(r, k).wait()
        return carry

    lax.fori_loop(0, tm, issue, 0)
    lax.fori_loop(0, tm, drain, 0)


def _dispatch(hp, dest_flat, n_slots):
    t, dw = hp.shape
    tm = _tile(t, 256)
    xs_init = jnp.zeros((n_slots, dw), U32)
    return pl.pallas_call(
        _dispatch_kernel,
        grid=(t // tm,),
        in_specs=[
            pl.BlockSpec((tm * TOP_K,), lambda i: (i,), memory_space=pltpu.SMEM),
            pl.BlockSpec((tm, dw), lambda i: (i, 0)),
            pl.BlockSpec(memory_space=pl.ANY),
        ],
        out_specs=pl.BlockSpec(memory_space=pl.ANY),
        out_shape=jax.ShapeDtypeStruct((n_slots, dw), U32),
        scratch_shapes=[pltpu.SemaphoreType.DMA(())],
        input_output_aliases={2: 0},
        compiler_params=_params(("arbitrary",)),
        name="moe_dispatch",
    )(dest_flat, hp, xs_init)


def _experts_kernel(te_ref, ns_ref, nu_ref, xs_ref, wgu_ref, wd_ref, bgu_ref, bd_ref, y_ref, x_ref, *, sub_rows):
    del te_ref, nu_ref
    i, c = pl.program_id(0), pl.program_id(1)
    n_sub = ns_ref[i]
    bm = y_ref.shape[0]
    half_words = xs_ref.shape[1]
    cw = wgu_ref.shape[1]
    hw = cw // 2

    @pl.when(c == 0)
    def _():
        words = xs_ref[...]
        x_ref[:, :half_words] = pltpu.bitcast(words << 16, F32).astype(BF16)
        x_ref[:, half_words:] = pltpu.bitcast(words & jnp.uint32(HI16), F32).astype(BF16)
        for sb in range(bm // sub_rows):
            y_ref[sb * sub_rows:(sb + 1) * sub_rows, :] = jnp.broadcast_to(
                jnp.where(sb < n_sub, bd_ref[...], 0.0), (sub_rows, y_ref.shape[1]))

    def sub_block(sb, wgu, wdi, even):
        rows = slice(sb * sub_rows, (sb + 1) * sub_rows)
        gu = jnp.dot(x_ref[rows, :], wgu, preferred_element_type=F32) + bgu_ref[...]
        gs = jnp.minimum(gu, SWIGLU_LIMIT)
        gate = gs * jax.nn.sigmoid(SWIGLU_ALPHA * gs)
        up = jnp.clip(gu, -SWIGLU_LIMIT, SWIGLU_LIMIT) + 1.0
        prod = []
        for v in range(cw // V7X_LANES):
            cols = slice(v * V7X_LANES, (v + 1) * V7X_LANES)
            prod.append(gate[:, cols] * pltpu.roll(up[:, cols], V7X_LANES - 1, 1))
        nv = hw // V7X_LANES
        act = jnp.concatenate(
            [jnp.where(even, prod[v], pltpu.roll(prod[nv + v], 1, 1)) for v in range(nv)], axis=1)
        y_ref[rows, :] += jnp.dot(act.astype(BF16), wdi, preferred_element_type=F32)

    for k in range(1, bm // sub_rows + 1):
        @pl.when(n_sub == k)
        def _(k=k):
            wgu = wgu_ref[...].astype(BF16)
            lo = pltpu.bitcast(wd_ref[:hw // 2, :].astype(BF16).astype(F32), U32) >> 16
            hi = pltpu.bitcast(wd_ref[hw // 2:, :].astype(BF16).astype(F32), U32) & jnp.uint32(HI16)
            wdi = pltpu.bitcast(lo | hi, BF16)
            even = (lax.broadcasted_iota(I32, (sub_rows, V7X_LANES), 1) & 1) == 0
            for sb in range(k):
                sub_block(sb, wgu, wdi, even)


def _experts(xs, tile_expert, tile_subs, n_used, w_gu, w_d, b_gu, b_d, *, bm, sub_rows):
    n_slots, dw = xs.shape
    n_exp, d, f2 = w_gu.shape
    cw = _tile(f2, MOE_GU_CHUNK)
    assert cw % (2 * V7X_LANES) == 0
    nt, nc = n_slots // bm, f2 // cw

    def tile_idx(i, nu):
        return jnp.minimum(i, nu[0] - 1)

    def chunk_idx(i, c, nu):
        return jnp.where(i < nu[0], c, nc - 1)

    grid_spec = pltpu.PrefetchScalarGridSpec(
        num_scalar_prefetch=3,
        grid=(nt, nc),
        in_specs=[
            pl.BlockSpec((bm, dw), lambda i, c, te, ns, nu: (tile_idx(i, nu), 0)),
            pl.BlockSpec((None, d, cw), lambda i, c, te, ns, nu: (te[tile_idx(i, nu)], 0, chunk_idx(i, c, nu))),
            pl.BlockSpec((None, cw // 2, d), lambda i, c, te, ns, nu: (te[tile_idx(i, nu)], chunk_idx(i, c, nu), 0)),
            pl.BlockSpec((None, 1, cw), lambda i, c, te, ns, nu: (te[tile_idx(i, nu)], 0, chunk_idx(i, c, nu))),
            pl.BlockSpec((None, 1, d), lambda i, c, te, ns, nu: (te[tile_idx(i, nu)], 0, 0)),
        ],
        out_specs=pl.BlockSpec((bm, d), lambda i, c, te, ns, nu: (i, 0)),
        scratch_shapes=[pltpu.VMEM((bm, d), BF16)],
    )
    return pl.pallas_call(
        functools.partial(_experts_kernel, sub_rows=sub_rows),
        grid_spec=grid_spec,
        out_shape=jax.ShapeDtypeStruct((n_slots, d), F32),
        compiler_params=_params(("arbitrary", "arbitrary")),
        name="moe_experts",
    )(tile_expert, tile_subs, n_used, xs, w_gu, w_d, b_gu.reshape(n_exp, 1, f2), b_d.reshape(n_exp, 1, d))


def _combine_kernel(dest_ref, x_ref, r_ref, y_ref, *rest, final_norm):
    if final_norm:
        g_ref, o_ref, buf_ref, sem = rest
    else:
        o_ref, buf_ref, sem = rest
    tm = x_ref.shape[0]

    def row_copy(r, k):
        return pltpu.make_async_copy(y_ref.at[pl.ds(dest_ref[r * TOP_K + k], 1), :],
                                     buf_ref.at[k, pl.ds(r, 1), :], sem)

    def issue(r, carry):
        for k in range(TOP_K):
            row_copy(r, k).start(priority=k % 2)
        return carry

    def drain(r, carry):
        for k in range(TOP_K):
            row_copy(r, k).wait()
        return carry

    lax.fori_loop(0, tm, issue, 0)
    lax.fori_loop(0, tm, drain, 0)

    acc = x_ref[...]
    for k in range(TOP_K):
        acc = acc + r_ref[:, 2 * TOP_K + k:2 * TOP_K + k + 1] * buf_ref[k]
    if final_norm:
        acc = acc * _rms_scale(acc) * g_ref[...]
    o_ref[...] = acc


def _combine(x, r, y, dest_flat, final_g=None):
    t, d = x.shape
    tm = _tile(t, 256)
    in_specs = [
        pl.BlockSpec((tm * TOP_K,), lambda i: (i,), memory_space=pltpu.SMEM),
        pl.BlockSpec((tm, d), lambda i: (i, 0)),
        pl.BlockSpec((tm, V7X_LANES), lambda i: (i, 0)),
        pl.BlockSpec(memory_space=pl.ANY),
    ]
    args = [dest_flat, x, r, y]
    if final_g is not None:
        in_specs.append(pl.BlockSpec((1, d), lambda i: (0, 0)))
        args.append(final_g.reshape(1, d))
    return pl.pallas_call(
        functools.partial(_combine_kernel, final_norm=final_g is not None),
        grid=(t // tm,),
        in_specs=in_specs,
        out_specs=pl.BlockSpec((tm, d), lambda i: (i, 0)),
        out_shape=jax.ShapeDtypeStruct((t, d), F32),
        scratch_shapes=[pltpu.VMEM((TOP_K, tm, d), F32), pltpu.SemaphoreType.DMA(())],
        compiler_params=_params(("arbitrary",)),
        name="moe_combine",
    )(*args)


def _moe(x, norm_g, router_w, router_b, w_gate_up, b_gate_up, w_down, b_down, final_g=None):
    t, d = x.shape
    n_exp = router_w.shape[1]
    bm = _tile(t, MOE_ROW_TILE)
    sub_rows = _tile(bm, MOE_SUB_ROWS)
    n_tiles = t * TOP_K // bm + n_exp
    hp, r, cnt = _router(x, norm_g, router_w, router_b)

    top_i = r[:, :TOP_K].astype(I32)
    rank = r[:, TOP_K:2 * TOP_K].astype(I32)
    counts = cnt[0].astype(I32)
    exp_tiles = (counts + bm - 1) // bm
    tile_end = jnp.cumsum(exp_tiles)
    tile_start = tile_end - exp_tiles
    dest_flat = (tile_start[top_i] * bm + rank).reshape(-1)
    n_used = tile_end[-1:].astype(I32)
    tile_ids = jnp.arange(n_tiles, dtype=I32)
    tile_expert = jnp.minimum(jnp.sum(tile_end[None, :] <= tile_ids[:, None], axis=1), n_exp - 1).astype(I32)
    tile_rows = jnp.clip(counts[tile_expert] - (tile_ids - tile_start[tile_expert]) * bm, 0, bm)
    tile_subs = jnp.where(tile_ids < n_used[0], (tile_rows + sub_rows - 1) // sub_rows, 0).astype(I32)

    xs = _dispatch(hp, dest_flat, n_tiles * bm)
    y = _experts(xs, tile_expert, tile_subs, n_used, w_gate_up, w_down, b_gate_up, b_down,
                 bm=bm, sub_rows=sub_rows)
    return _combine(x, r, y, dest_flat, final_g)


def _rope_tables(seq, dh, rot_dim, q_scale):
    half = rot_dim // 2
    inv_freq = ROPE_THETA ** (-jnp.arange(0, rot_dim, 2, dtype=F32) / rot_dim)
    ang = jnp.arange(seq, dtype=F32)[:, None] * inv_freq[None, :]
    cos, sin = jnp.cos(ang), jnp.sin(ang)
    ones = jnp.ones((seq, dh - rot_dim), F32)
    zeros_rest = jnp.zeros((seq, dh - rot_dim), F32)
    zeros_half = jnp.zeros((seq, half), F32)
    c = jnp.concatenate([cos, cos, ones], axis=1)
    s1 = jnp.concatenate([zeros_half, sin, zeros_rest], axis=1)
    s2 = jnp.concatenate([-sin, zeros_half, zeros_rest], axis=1)
    k_tab = jnp.stack([jnp.tile(a, (1, 2)) for a in (c, s1, s2)])
    return jnp.stack([k_tab * q_scale, k_tab])


def _gmlp_layer(x, norm_g, w_in, ln_g, ln_b, w_s, b_s, w_out):
    z = _norm_matmul(x, norm_g, w_in.astype(BF16))
    y = _gmlp_gate(z, ln_g, ln_b, w_s, b_s)
    return _matmul_residual(y, w_out.astype(BF16), x)


def _attn_layer(x, norm_g, w_qkv, lq1, lk1, lq2, lk2, subln_g, w_o, lambda_init, *, batch, seq):
    d = x.shape[1]
    dh = lq1.shape[0]
    n_heads = d // (2 * dh)
    rot_dim = dh // 4
    tab = _rope_tables(seq, dh, rot_dim, dh ** -0.5)
    qkv = _norm_matmul(x, norm_g, w_qkv.astype(BF16), rope_tab=tab, seq=seq,
                       n_rope_cols=2 * d, rot_half=rot_dim // 2)
    lam = jnp.exp(jnp.sum(lq1 * lk1)) - jnp.exp(jnp.sum(lq2 * lk2)) + lambda_init
    o = _diff_attention(qkv, lam.astype(F32), subln_g, batch=batch, seq=seq, n_heads=n_heads,
                        out_scale=1.0 - lambda_init)
    return _matmul_residual(o, w_o.astype(BF16), x)


def kernel(x, attn_norm_g, ffn_norm_g, final_norm_g, a_w_in, a_ln_g, a_ln_b, a_w_s, a_b_s, a_w_out,
           b_w_qkv, b_lambda_q1, b_lambda_k1, b_lambda_q2, b_lambda_k2, b_subln_g, b_w_o,
           router_w, router_b, w_gate_up, b_gate_up, w_down, b_down):
    batch, seq, d = x.shape
    depth = attn_norm_g.shape[0]
    n_mixers = 2
    h = x.reshape(batch * seq, d)
    for i in range(depth):
        j = i // n_mixers
        if i % n_mixers == 0:
            h = _gmlp_layer(h, attn_norm_g[i], a_w_in[j], a_ln_g[j], a_ln_b[j], a_w_s[j], a_b_s[j], a_w_out[j])
        else:
            lambda_init = 0.8 - 0.6 * math.exp(-0.3 * i)
            h = _attn_layer(h, attn_norm_g[i], b_w_qkv[j], b_lambda_q1[j], b_lambda_k1[j], b_lambda_q2[j],
                            b_lambda_k2[j], b_subln_g[j], b_w_o[j], lambda_init, batch=batch, seq=seq)
        h = _moe(h, ffn_norm_g[i], router_w[i], router_b[i], w_gate_up[i], b_gate_up[i], w_down[i], b_down[i],
                 final_g=final_norm_g if i == depth - 1 else None)
    return h.reshape(batch, seq, d)
```

```python
import functools
import math

import jax
import jax.numpy as jnp
from jax import lax
from jax.experimental import pallas as pl
from jax.experimental.pallas import tpu as pltpu

F32 = jnp.float32
BF16 = jnp.bfloat16
I32 = jnp.int32
U32 = jnp.uint32

EPS = 1e-5
TOP_K = 4
ROPE_THETA = 500000.0
SWIGLU_LIMIT = 7.0
SWIGLU_ALPHA = 1.702

V7X_LANES = 128
V7X_VMEM_BYTES = 64 * 1024 * 1024
VMEM_LIMIT_BYTES = V7X_VMEM_BYTES * 7 // 8
HI16 = 0xFFFF0000

ATTN_ROW_CHUNK = 512
MOE_ROW_TILE = 1024
MOE_SUB_ROWS = 256
MOE_GU_CHUNK = 512


def _tile(dim, want):
    t = min(dim, want)
    while dim % t:
        t //= 2
    return t


def _params(semantics):
    return pltpu.CompilerParams(dimension_semantics=semantics, vmem_limit_bytes=VMEM_LIMIT_BYTES)


def _rms_scale(x):
    return lax.rsqrt(jnp.mean(x * x, axis=-1, keepdims=True) + EPS)


def _norm_matmul_gelu_kernel(x_ref, g_ref, w_ref, o_ref, xn_ref):
    @pl.when(pl.program_id(1) == 0)
    def _():
        x = x_ref[...]
        xn_ref[...] = (x * _rms_scale(x) * g_ref[...]).astype(BF16)

    y = jnp.dot(xn_ref[...], w_ref[...], preferred_element_type=F32)
    o_ref[...] = jax.nn.gelu(y).astype(o_ref.dtype)


def _norm_matmul_rope_kernel(x_ref, g_ref, w_ref, tab_ref, o_ref, xn_ref, *, n_rope_tiles, shift):
    j = pl.program_id(1)

    @pl.when(j == 0)
    def _():
        x = x_ref[...]
        xn_ref[...] = (x * _rms_scale(x) * g_ref[...]).astype(BF16)

    y = jnp.dot(xn_ref[...], w_ref[...], preferred_element_type=F32)

    @pl.when(j < n_rope_tiles)
    def _():
        c, s1, s2 = tab_ref[0], tab_ref[1], tab_ref[2]
        for h in range(y.shape[1] // V7X_LANES):
            cols = slice(h * V7X_LANES, (h + 1) * V7X_LANES)
            yh = y[:, cols]
            r = (yh * c + pltpu.roll(yh, shift, 1) * s1
                 + pltpu.roll(yh, V7X_LANES - shift, 1) * s2)
            o_ref[:, cols] = r.astype(o_ref.dtype)

    @pl.when(j >= n_rope_tiles)
    def _():
        o_ref[...] = y.astype(o_ref.dtype)


def _norm_matmul(x, g, w, *, rope_tab=None, seq=None, n_rope_cols=0, rot_half=0):
    t, k = x.shape
    n = w.shape[1]
    tm = _tile(t if seq is None else seq, 1024)
    tn = _tile(n if rope_tab is None else n_rope_cols // 2, 1024)
    grid = (t // tm, n // tn)
    in_specs = [
        pl.BlockSpec((tm, k), lambda i, j: (i, 0)),
        pl.BlockSpec((1, k), lambda i, j: (0, 0)),
        pl.BlockSpec((k, tn), lambda i, j: (0, j)),
    ]
    args = [x, g.reshape(1, k), w]
    if rope_tab is None:
        body = _norm_matmul_gelu_kernel
    else:
        assert n_rope_cols % (2 * tn) == 0 and tn % V7X_LANES == 0
        n_rope_tiles = n_rope_cols // tn
        s_tiles = seq // tm
        in_specs.append(pl.BlockSpec(
            (None, 3, tm, V7X_LANES),
            lambda i, j: (jnp.minimum(j // (n_rope_tiles // 2), 1), 0, i % s_tiles, 0)))
        args.append(rope_tab)
        body = functools.partial(_norm_matmul_rope_kernel, n_rope_tiles=n_rope_tiles, shift=rot_half)
    return pl.pallas_call(
        body,
        grid=grid,
        in_specs=in_specs,
        out_specs=pl.BlockSpec((tm, tn), lambda i, j: (i, j)),
        out_shape=jax.ShapeDtypeStruct((t, n), BF16),
        scratch_shapes=[pltpu.VMEM((tm, k), BF16)],
        compiler_params=_params(("parallel", "arbitrary")),
        name="norm_matmul_gelu" if rope_tab is None else "norm_matmul_rope",
    )(*args)


def _matmul_residual_kernel(a_ref, w_ref, r_ref, o_ref):
    o_ref[...] = r_ref[...] + jnp.dot(a_ref[...], w_ref[...], preferred_element_type=F32)


def _matmul_residual(a, w, res):
    t, k = a.shape
    n = w.shape[1]
    tm, tn = _tile(t, 1024), _tile(n, 1024)
    return pl.pallas_call(
        _matmul_residual_kernel,
        grid=(t // tm, n // tn),
        in_specs=[
            pl.BlockSpec((tm, k), lambda i, j: (i, 0)),
            pl.BlockSpec((k, tn), lambda i, j: (0, j)),
            pl.BlockSpec((tm, tn), lambda i, j: (i, j)),
        ],
        out_specs=pl.BlockSpec((tm, tn), lambda i, j: (i, j)),
        out_shape=jax.ShapeDtypeStruct((t, n), F32),
        compiler_params=_params(("parallel", "parallel")),
        name="matmul_residual",
    )(a, w, res)


def _gmlp_gate_kernel(u_ref, v_ref, lng_ref, lnb_ref, ws_ref, bst_ref, y_ref, vn_ref):
    n_groups, chunk, _ = ws_ref.shape
    tm, width = u_ref.shape
    gd = width // n_groups
    v = v_ref[...].astype(F32)
    vc = v - jnp.mean(v, axis=-1, keepdims=True)
    inv = lax.rsqrt(jnp.mean(vc * vc, axis=-1, keepdims=True) + EPS)
    vn_ref[...] = (vc * inv * lng_ref[...] + lnb_ref[...]).astype(BF16)
    causal = (lax.broadcasted_iota(I32, (chunk, chunk), 0) >= lax.broadcasted_iota(I32, (chunk, chunk), 1))
    for g in range(n_groups):
        w = jnp.where(causal, ws_ref[g], 0.0).astype(BF16)
        bias = bst_ref[:, g:g + 1]
        cols = slice(g * gd, (g + 1) * gd)
        for ci in range(tm // chunk):
            rows = slice(ci * chunk, (ci + 1) * chunk)
            sv = jnp.dot(w, vn_ref[rows, cols], preferred_element_type=F32) + bias
            y_ref[rows, cols] = (u_ref[rows, cols].astype(F32) * sv).astype(y_ref.dtype)


def _gmlp_gate(z, ln_g, ln_b, w_s, b_s):
    t, w2 = z.shape
    width = w2 // 2
    n_groups, chunk, _ = w_s.shape
    tm = _tile(t, 4 * chunk)
    return pl.pallas_call(
        _gmlp_gate_kernel,
        grid=(t // tm,),
        in_specs=[
            pl.BlockSpec((tm, width), lambda i: (i, 0)),
            pl.BlockSpec((tm, width), lambda i: (i, 1)),
            pl.BlockSpec((1, width), lambda i: (0, 0)),
            pl.BlockSpec((1, width), lambda i: (0, 0)),
            pl.BlockSpec((n_groups, chunk, chunk), lambda i: (0, 0, 0)),
            pl.BlockSpec((chunk, n_groups), lambda i: (0, 0)),
        ],
        out_specs=pl.BlockSpec((tm, width), lambda i: (i, 0)),
        out_shape=jax.ShapeDtypeStruct((t, width), BF16),
        scratch_shapes=[pltpu.VMEM((tm, width), BF16)],
        compiler_params=_params(("parallel",)),
        name="gmlp_gate",
    )(z, z, ln_g.reshape(1, width), ln_b.reshape(1, width), w_s, b_s.T)


def _diff_attn_kernel(lam_ref, q_ref, k_ref, v_ref, g_ref, o_ref, qs_ref, s_ref, m_ref, acc_ref, *, out_scale):
    qi = pl.program_id(2)
    tq, dh2 = q_ref.shape
    half = dh2 // 2
    row_chunk = _tile(tq, ATTN_ROW_CHUNK)

    q = q_ref[...]
    lane = lax.broadcasted_iota(I32, q.shape, 1)
    qs_ref[:tq, :] = jnp.where(lane < half, q, jnp.zeros_like(q))
    qs_ref[tq:, :] = jnp.where(lane >= half, q, jnp.zeros_like(q))
    m_ref[...] = jnp.full_like(m_ref, -jnp.inf)
    acc_ref[...] = jnp.zeros_like(acc_ref)

    def kv_rows(kj):
        return pl.ds(pl.multiple_of(kj * tq, tq), tq)

    def scores(kj, slot):
        s_ref[slot] = lax.dot_general(qs_ref[...], k_ref[kv_rows(kj), :], (((1,), (1,)), ((), ())),
                                      preferred_element_type=F32)

    def softmax_pv(kj, slot, masked):
        v = v_ref[kv_rows(kj), :]
        v1 = jnp.concatenate([v, jnp.ones_like(v)], axis=1)
        for r in range(2 * tq // row_chunk):
            rs = slice(r * row_chunk, (r + 1) * row_chunk)
            s = s_ref[slot, rs, :]
            if masked:
                row = lax.broadcasted_iota(I32, s.shape, 0) + (r * row_chunk) % tq
                col = lax.broadcasted_iota(I32, s.shape, 1)
                s = jnp.where(col <= row, s, -jnp.inf)
            m_old = m_ref[rs, :]
            m_new = jnp.maximum(m_old, jnp.max(s, axis=-1, keepdims=True))
            p = jnp.exp((s - m_new).astype(BF16))
            alpha = jnp.exp(m_old - m_new)
            acc_ref[rs, :] = alpha * acc_ref[rs, :] + jnp.dot(p, v1, preferred_element_type=F32)
            m_ref[rs, :] = m_new

    scores(0, 0)

    def off_diagonal(kj, carry):
        slot = kj % 2
        softmax_pv(kj, slot, masked=False)
        scores(kj + 1, 1 - slot)
        return carry

    lax.fori_loop(0, qi, off_diagonal, 0)
    softmax_pv(qi, qi % 2, masked=True)

    o = acc_ref[:, :dh2] / acc_ref[:, dh2:]
    o = o[:tq] - lam_ref[0] * o[tq:]
    o = o * _rms_scale(o) * g_ref[...] * out_scale
    o_ref[...] = o.astype(o_ref.dtype)


def _diff_attention(qkv, lam, subln_g, *, batch, seq, n_heads, out_scale):
    t = qkv.shape[0]
    dh2 = qkv.shape[1] // (3 * n_heads)
    tq = _tile(seq, 512)
    nq = seq // tq
    return pl.pallas_call(
        functools.partial(_diff_attn_kernel, out_scale=out_scale),
        grid=(batch, n_heads, nq),
        in_specs=[
            pl.BlockSpec(memory_space=pltpu.SMEM),
            pl.BlockSpec((tq, dh2), lambda b, h, i: (b * nq + i, h)),
            pl.BlockSpec((seq, dh2), lambda b, h, i: (b, n_heads + h)),
            pl.BlockSpec((seq, dh2), lambda b, h, i: (b, 2 * n_heads + h)),
            pl.BlockSpec((1, dh2), lambda b, h, i: (0, 0)),
        ],
        out_specs=pl.BlockSpec((tq, dh2), lambda b, h, i: (b * nq + i, h)),
        out_shape=jax.ShapeDtypeStruct((t, n_heads * dh2), BF16),
        scratch_shapes=[
            pltpu.VMEM((2 * tq, dh2), BF16),
            pltpu.VMEM((2, 2 * tq, tq), F32),
            pltpu.VMEM((2 * tq, 1), F32),
            pltpu.VMEM((2 * tq, 2 * dh2), F32),
        ],
        compiler_params=_params(("parallel", "parallel", "arbitrary")),
        name="diff_attention",
    )(lam.reshape(1), qkv, qkv, qkv, subln_g.reshape(1, dh2))


def _router_kernel(x_ref, g_ref, rw_ref, rb_ref, hp_ref, r_ref, cnt_ref, carry_ref):
    @pl.when(pl.program_id(0) == 0)
    def _():
        carry_ref[...] = jnp.zeros_like(carry_ref)

    x = x_ref[...]
    tm, d = x.shape
    n_exp = rw_ref.shape[1]
    h = x * _rms_scale(x) * g_ref[...]
    h_hi = h.astype(BF16)
    h_hi32 = h_hi.astype(F32)
    half = d // 2
    lo_word = pltpu.bitcast(h_hi32[:, :half], U32) >> 16
    hi_word = pltpu.bitcast(h_hi32[:, half:], U32) & jnp.uint32(HI16)
    hp_ref[...] = lo_word | hi_word

    h_lo = (h - h_hi32).astype(BF16)
    rw = rw_ref[...]
    w_hi = rw.astype(BF16)
    w_lo = (rw - w_hi.astype(F32)).astype(BF16)
    logits = (jnp.dot(h_hi, w_hi, preferred_element_type=F32)
              + jnp.dot(h_lo, w_hi, preferred_element_type=F32)
              + jnp.dot(h_hi, w_lo, preferred_element_type=F32)) + rb_ref[...]

    e_iota = lax.broadcasted_iota(I32, (tm, n_exp), 1).astype(F32)
    work = logits
    vals, onehots, idxs = [], [], []
    for _ in range(TOP_K):
        m = jnp.max(work, axis=-1, keepdims=True)
        idx = jnp.min(jnp.where(work == m, e_iota, float(n_exp)), axis=-1, keepdims=True)
        hit = e_iota == idx
        vals.append(m)
        idxs.append(idx)
        onehots.append(hit)
        work = jnp.where(hit, -jnp.inf, work)
    exps = [jnp.exp(v - vals[0]) for v in vals]
    denom = functools.reduce(lambda a, b: a + b, exps)

    chosen = functools.reduce(jnp.logical_or, onehots)
    chosen_f = jnp.where(chosen, 1.0, 0.0)
    earlier = (lax.broadcasted_iota(I32, (tm, tm), 0) > lax.broadcasted_iota(I32, (tm, tm), 1))
    before = jnp.dot(jnp.where(earlier, 1.0, 0.0).astype(BF16), chosen_f.astype(BF16),
                     preferred_element_type=F32) + carry_ref[...]
    carry_ref[...] = carry_ref[...] + jnp.sum(chosen_f, axis=0, keepdims=True)
    cnt_ref[...] = carry_ref[...]

    lane = lax.broadcasted_iota(I32, r_ref.shape, 1)
    out = jnp.zeros(r_ref.shape, F32)
    for k in range(TOP_K):
        rank = jnp.sum(jnp.where(onehots[k], before, 0.0), axis=-1, keepdims=True)
        out = jnp.where(lane == k, idxs[k], out)
        out = jnp.where(lane == TOP_K + k, rank, out)
        out = jnp.where(lane == 2 * TOP_K + k, exps[k] / denom, out)
    r_ref[...] = out


def _router(x, g, router_w, router_b):
    t, d = x.shape
    n_exp = router_w.shape[1]
    tm = _tile(t, 512)
    return pl.pallas_call(
        _router_kernel,
        grid=(t // tm,),
        in_specs=[
            pl.BlockSpec((tm, d), lambda i: (i, 0)),
            pl.BlockSpec((1, d), lambda i: (0, 0)),
            pl.BlockSpec((d, n_exp), lambda i: (0, 0)),
            pl.BlockSpec((1, n_exp), lambda i: (0, 0)),
        ],
        out_specs=[
            pl.BlockSpec((tm, d // 2), lambda i: (i, 0)),
            pl.BlockSpec((tm, V7X_LANES), lambda i: (i, 0)),
            pl.BlockSpec((1, n_exp), lambda i: (0, 0)),
        ],
        out_shape=[
            jax.ShapeDtypeStruct((t, d // 2), U32),
            jax.ShapeDtypeStruct((t, V7X_LANES), F32),
            jax.ShapeDtypeStruct((1, n_exp), F32),
        ],
        scratch_shapes=[pltpu.VMEM((1, n_exp), F32)],
        compiler_params=_params(("arbitrary",)),
        name="moe_router",
    )(x, g.reshape(1, d), router_w, router_b.reshape(1, n_exp))


def _dispatch_kernel(dest_ref, hp_ref, xs_init_ref, xs_ref, sem):
    del xs_init_ref
    tm = hp_ref.shape[0]

    def row_copy(r, k):
        return pltpu.make_async_copy(hp_ref.at[pl.ds(r, 1), :],
                                     xs_ref.at[pl.ds(dest_ref[r * TOP_K + k], 1), :], sem)

    def issue(r, carry):
        for k in range(TOP_K):
            row_copy(r, k).start(priority=k % 2)
        return carry

    def drain(r, carry):
        for k in range(TOP_K):
            row_copy(r, k).wait()
        return carry

    lax.fori_loop(0, tm, issue, 0)
    lax.fori_loop(0, tm, drain, 0)


def _dispatch(hp, dest_flat, n_slots):
    t, dw = hp.shape
    tm = _tile(t, 256)
    xs_init = jnp.zeros((n_slots, dw), U32)
    return pl.pallas_call(
        _dispatch_kernel,
        grid=(t // tm,),
        in_specs=[
            pl.BlockSpec((tm * TOP_K,), lambda i: (i,), memory_space=pltpu.SMEM),
            pl.BlockSpec((tm, dw), lambda i: (i, 0)),
            pl.BlockSpec(memory_space=pl.ANY),
        ],
        out_specs=pl.BlockSpec(memory_space=pl.ANY),
        out_shape=jax.ShapeDtypeStruct((n_slots, dw), U32),
        scratch_shapes=[pltpu.SemaphoreType.DMA(())],
        input_output_aliases={2: 0},
        compiler_params=_params(("arbitrary",)),
        name="moe_dispatch",
    )(dest_flat, hp, xs_init)


def _experts_kernel(te_ref, ns_ref, nu_ref, xs_ref, wgu_ref, wd_ref, bgu_ref, bd_ref, y_ref, x_ref, *, sub_rows):
    del te_ref, nu_ref
    i, c = pl.program_id(0), pl.program_id(1)
    n_sub = ns_ref[i]
    bm = y_ref.shape[0]
    half_words = xs_ref.shape[1]
    cw = wgu_ref.shape[1]
    hw = cw // 2

    @pl.when(c == 0)
    def _():
        words = xs_ref[...]
        x_ref[:, :half_words] = pltpu.bitcast(words << 16, F32).astype(BF16)
        x_ref[:, half_words:] = pltpu.bitcast(words & jnp.uint32(HI16), F32).astype(BF16)
        for sb in range(bm // sub_rows):
            y_ref[sb * sub_rows:(sb + 1) * sub_rows, :] = jnp.broadcast_to(
                jnp.where(sb < n_sub, bd_ref[...], 0.0), (sub_rows, y_ref.shape[1]))

    def token_rows(m_rows):
        wgu = wgu_ref[...].astype(BF16)
        lo = pltpu.bitcast(wd_ref[:hw // 2, :].astype(BF16).astype(F32), U32) >> 16
        hi = pltpu.bitcast(wd_ref[hw // 2:, :].astype(BF16).astype(F32), U32) & jnp.uint32(HI16)
        wdi = pltpu.bitcast(lo | hi, BF16)
        rows = slice(0, m_rows)
        even = (lax.broadcasted_iota(I32, (m_rows, V7X_LANES), 1) & 1) == 0
        gu = jnp.dot(x_ref[rows, :], wgu, preferred_element_type=F32) + bgu_ref[...]
        gs = jnp.minimum(gu, SWIGLU_LIMIT)
        gate = gs * jax.nn.sigmoid(SWIGLU_ALPHA * gs)
        up = jnp.clip(gu, -SWIGLU_LIMIT, SWIGLU_LIMIT) + 1.0
        prod = []
        for v in range(cw // V7X_LANES):
            cols = slice(v * V7X_LANES, (v + 1) * V7X_LANES)
            prod.append(gate[:, cols] * pltpu.roll(up[:, cols], V7X_LANES - 1, 1))
        nv = hw // V7X_LANES
        act = jnp.concatenate(
            [jnp.where(even, prod[v], pltpu.roll(prod[nv + v], 1, 1)) for v in range(nv)], axis=1)
        y_ref[rows, :] += jnp.dot(act.astype(BF16), wdi, preferred_element_type=F32)

    for k in range(1, bm // sub_rows + 1):
        @pl.when(n_sub == k)
        def _(k=k):
            token_rows(k * sub_rows)


def _experts(xs, tile_expert, tile_subs, n_used, w_gu, w_d, b_gu, b_d, *, layer, bm, sub_rows):
    n_slots, dw = xs.shape
    depth, n_exp, d, f2 = w_gu.shape
    cw = _tile(f2, MOE_GU_CHUNK)
    assert cw % (2 * V7X_LANES) == 0
    nt, nc = n_slots // bm, f2 // cw

    def tile_idx(i, nu):
        return jnp.minimum(i, nu[0] - 1)

    def chunk_idx(i, c, nu):
        return jnp.where(i < nu[0], c, nc - 1)

    def expert(i, te, nu):
        return te[tile_idx(i, nu)]

    grid_spec = pltpu.PrefetchScalarGridSpec(
        num_scalar_prefetch=3,
        grid=(nt, nc),
        in_specs=[
            pl.BlockSpec((bm, dw), lambda i, c, te, ns, nu: (tile_idx(i, nu), 0)),
            pl.BlockSpec((None, None, d, cw),
                         lambda i, c, te, ns, nu: (layer, expert(i, te, nu), 0, chunk_idx(i, c, nu))),
            pl.BlockSpec((None, None, cw // 2, d),
                         lambda i, c, te, ns, nu: (layer, expert(i, te, nu), chunk_idx(i, c, nu), 0)),
            pl.BlockSpec((None, None, 1, cw),
                         lambda i, c, te, ns, nu: (layer, expert(i, te, nu), 0, chunk_idx(i, c, nu))),
            pl.BlockSpec((None, None, 1, d), lambda i, c, te, ns, nu: (layer, expert(i, te, nu), 0, 0)),
        ],
        out_specs=pl.BlockSpec((bm, d), lambda i, c, te, ns, nu: (i, 0)),
        scratch_shapes=[pltpu.VMEM((bm, d), BF16)],
    )
    return pl.pallas_call(
        functools.partial(_experts_kernel, sub_rows=sub_rows),
        grid_spec=grid_spec,
        out_shape=jax.ShapeDtypeStruct((n_slots, d), F32),
        compiler_params=_params(("arbitrary", "arbitrary")),
        name="moe_experts",
    )(tile_expert, tile_subs, n_used, xs, w_gu, w_d,
      b_gu.reshape(depth, n_exp, 1, f2), b_d.reshape(depth, n_exp, 1, d))


def _combine_kernel(dest_ref, x_ref, r_ref, y_ref, *rest, final_norm):
    if final_norm:
        g_ref, o_ref, buf_ref, sem = rest
    else:
        o_ref, buf_ref, sem = rest
    tm = x_ref.shape[0]

    def row_copy(r, k):
        return pltpu.make_async_copy(y_ref.at[pl.ds(dest_ref[r * TOP_K + k], 1), :],
                                     buf_ref.at[k, pl.ds(r, 1), :], sem)

    def issue(r, carry):
        for k in range(TOP_K):
            row_copy(r, k).start(priority=k % 2)
        return carry

    def drain(r, carry):
        for k in range(TOP_K):
            row_copy(r, k).wait()
        return carry

    lax.fori_loop(0, tm, issue, 0)
    lax.fori_loop(0, tm, drain, 0)

    acc = x_ref[...]
    for k in range(TOP_K):
        acc = acc + r_ref[:, 2 * TOP_K + k:2 * TOP_K + k + 1] * buf_ref[k]
    if final_norm:
        acc = acc * _rms_scale(acc) * g_ref[...]
    o_ref[...] = acc


def _combine(x, r, y, dest_flat, final_g=None):
    t, d = x.shape
    tm = _tile(t, 256)
    in_specs = [
        pl.BlockSpec((tm * TOP_K,), lambda i: (i,), memory_space=pltpu.SMEM),
        pl.BlockSpec((tm, d), lambda i: (i, 0)),
        pl.BlockSpec((tm, V7X_LANES), lambda i: (i, 0)),
        pl.BlockSpec(memory_space=pl.ANY),
    ]
    args = [dest_flat, x, r, y]
    if final_g is not None:
        in_specs.append(pl.BlockSpec((1, d), lambda i: (0, 0)))
        args.append(final_g.reshape(1, d))
    return pl.pallas_call(
        functools.partial(_combine_kernel, final_norm=final_g is not None),
        grid=(t // tm,),
        in_specs=in_specs,
        out_specs=pl.BlockSpec((tm, d), lambda i: (i, 0)),
        out_shape=jax.ShapeDtypeStruct((t, d), F32),
        scratch_shapes=[pltpu.VMEM((TOP_K, tm, d), F32), pltpu.SemaphoreType.DMA(())],
        compiler_params=_params(("arbitrary",)),
        name="moe_combine",
    )(*args)


def _moe(x, norm_g, router_w, router_b, w_gate_up, b_gate_up, w_down, b_down, *, layer, final_g=None):
    t, d = x.shape
    n_exp = router_w.shape[1]
    bm = _tile(t, MOE_ROW_TILE)
    sub_rows = _tile(bm, MOE_SUB_ROWS)
    n_tiles = t * TOP_K // bm + n_exp
    hp, r, cnt = _router(x, norm_g, router_w, router_b)

    top_i = r[:, :TOP_K].astype(I32)
    rank = r[:, TOP_K:2 * TOP_K].astype(I32)
    counts = cnt[0].astype(I32)
    exp_tiles = (counts + bm - 1) // bm
    tile_end = jnp.cumsum(exp_tiles)
    tile_start = tile_end - exp_tiles
    dest_flat = (tile_start[top_i] * bm + rank).reshape(-1)
    n_used = tile_end[-1:].astype(I32)
    tile_ids = jnp.arange(n_tiles, dtype=I32)
    tile_expert = jnp.minimum(jnp.sum(tile_end[None, :] <= tile_ids[:, None], axis=1), n_exp - 1).astype(I32)
    tile_rows = jnp.clip(counts[tile_expert] - (tile_ids - tile_start[tile_expert]) * bm, 0, bm)
    tile_subs = jnp.where(tile_ids < n_used[0], (tile_rows + sub_rows - 1) // sub_rows, 0).astype(I32)

    xs = _dispatch(hp, dest_flat, n_tiles * bm)
    y = _experts(xs, tile_expert, tile_subs, n_used, w_gate_up, w_down, b_gate_up, b_down,
                 layer=layer, bm=bm, sub_rows=sub_rows)
    return _combine(x, r, y, dest_flat, final_g)


def _rope_tables(seq, dh, rot_dim, q_scale):
    half = rot_dim // 2
    inv_freq = ROPE_THETA ** (-jnp.arange(0, rot_dim, 2, dtype=F32) / rot_dim)
    ang = jnp.arange(seq, dtype=F32)[:, None] * inv_freq[None, :]
    cos, sin = jnp.cos(ang), jnp.sin(ang)
    ones = jnp.ones((seq, dh - rot_dim), F32)
    zeros_rest = jnp.zeros((seq, dh - rot_dim), F32)
    zeros_half = jnp.zeros((seq, half), F32)
    c = jnp.concatenate([cos, cos, ones], axis=1)
    s1 = jnp.concatenate([zeros_half, sin, zeros_rest], axis=1)
    s2 = jnp.concatenate([-sin, zeros_half, zeros_rest], axis=1)
    k_tab = jnp.stack([jnp.tile(a, (1, 2)) for a in (c, s1, s2)])
    return jnp.stack([k_tab * q_scale, k_tab])


def _gmlp_layer(x, norm_g, w_in, ln_g, ln_b, w_s, b_s, w_out):
    z = _norm_matmul(x, norm_g, w_in.astype(BF16))
    y = _gmlp_gate(z, ln_g, ln_b, w_s, b_s)
    return _matmul_residual(y, w_out.astype(BF16), x)


def _attn_layer(x, norm_g, w_qkv, lq1, lk1, lq2, lk2, subln_g, w_o, lambda_init, *, batch, seq):
    d = x.shape[1]
    dh = lq1.shape[0]
    n_heads = d // (2 * dh)
    rot_dim = dh // 4
    tab = _rope_tables(seq, dh, rot_dim, dh ** -0.5)
    qkv = _norm_matmul(x, norm_g, w_qkv.astype(BF16), rope_tab=tab, seq=seq,
                       n_rope_cols=2 * d, rot_half=rot_dim // 2)
    lam = jnp.exp(jnp.sum(lq1 * lk1)) - jnp.exp(jnp.sum(lq2 * lk2)) + lambda_init
    o = _diff_attention(qkv, lam.astype(F32), subln_g, batch=batch, seq=seq, n_heads=n_heads,
                        out_scale=1.0 - lambda_init)
    return _matmul_residual(o, w_o.astype(BF16), x)


def kernel(x, attn_norm_g, ffn_norm_g, final_norm_g, a_w_in, a_ln_g, a_ln_b, a_w_s, a_b_s, a_w_out,
           b_w_qkv, b_lambda_q1, b_lambda_k1, b_lambda_q2, b_lambda_k2, b_subln_g, b_w_o,
           router_w, router_b, w_gate_up, b_gate_up, w_down, b_down):
    batch, seq, d = x.shape
    depth = attn_norm_g.shape[0]
    n_mixers = 2
    h = x.reshape(batch * seq, d)
    for i in range(depth):
        j = i // n_mixers
        if i % n_mixers == 0:
            h = _gmlp_layer(h, attn_norm_g[i], a_w_in[j], a_ln_g[j], a_ln_b[j], a_w_s[j], a_b_s[j], a_w_out[j])
        else:
            lambda_init = 0.8 - 0.6 * math.exp(-0.3 * i)
            h = _attn_layer(h, attn_norm_g[i], b_w_qkv[j], b_lambda_q1[j], b_lambda_k1[j], b_lambda_q2[j],
                            b_lambda_k2[j], b_subln_g[j], b_w_o[j], lambda_init, batch=batch, seq=seq)
        h = _moe(h, ffn_norm_g[i], router_w[i], router_b[i], w_gate_up, b_gate_up, w_down, b_down,
                 layer=i, final_g=final_norm_g if i == depth - 1 else None)
    return h.reshape(batch, seq, d)
```

```python
import functools
import math

import jax
import jax.numpy as jnp
from jax import lax
from jax.experimental import pallas as pl
from jax.experimental.pallas import tpu as pltpu

F32 = jnp.float32
BF16 = jnp.bfloat16
I32 = jnp.int32
U32 = jnp.uint32

EPS = 1e-5
TOP_K = 4
ROPE_THETA = 500000.0
SWIGLU_LIMIT = 7.0
SWIGLU_ALPHA = 1.702

V7X_LANES = 128
V7X_VMEM_BYTES = 64 * 1024 * 1024
VMEM_LIMIT_BYTES = V7X_VMEM_BYTES * 7 // 8
HI16 = 0xFFFF0000

ATTN_HEADS_PER_STEP = 1
ATTN_ROW_CHUNK = 512
MOE_ROW_TILE = 1024
MOE_SUB_ROWS = 128
MOE_GU_CHUNK = 512


def _tile(dim, want):
    t = min(dim, want)
    while dim % t:
        t //= 2
    return t


def _params(semantics):
    return pltpu.CompilerParams(dimension_semantics=semantics, vmem_limit_bytes=VMEM_LIMIT_BYTES)


def _rms_scale(x):
    return lax.rsqrt(jnp.mean(x * x, axis=-1, keepdims=True) + EPS)


def _norm_matmul_gelu_kernel(x_ref, g_ref, w_ref, o_ref, xn_ref):
    @pl.when(pl.program_id(1) == 0)
    def _():
        x = x_ref[...]
        xn_ref[...] = (x * _rms_scale(x) * g_ref[...]).astype(BF16)

    y = jnp.dot(xn_ref[...], w_ref[...], preferred_element_type=F32)
    o_ref[...] = jax.nn.gelu(y).astype(o_ref.dtype)


def _norm_matmul_rope_kernel(x_ref, g_ref, w_ref, tab_ref, o_ref, xn_ref, *, shift):
    @pl.when(pl.program_id(1) == 0)
    def _():
        x = x_ref[...]
        xn_ref[...] = (x * _rms_scale(x) * g_ref[...]).astype(BF16)

    y = jnp.dot(xn_ref[...], w_ref[...], preferred_element_type=F32)
    c, s1, s2 = tab_ref[0], tab_ref[1], tab_ref[2]
    for h in range(y.shape[1] // V7X_LANES):
        cols = slice(h * V7X_LANES, (h + 1) * V7X_LANES)
        yh = y[:, cols]
        r = (yh * c + pltpu.roll(yh, shift, 1) * s1
             + pltpu.roll(yh, V7X_LANES - shift, 1) * s2)
        o_ref[:, cols] = r.astype(o_ref.dtype)


def _norm_matmul(x, g, w, *, rope_tab=None, seq=None, rot_half=0):
    t, k = x.shape
    n = w.shape[1]
    tm = _tile(t if seq is None else seq, 1024)
    tn = _tile(n if rope_tab is None else n // 3, 1024)
    grid = (t // tm, n // tn)
    in_specs = [
        pl.BlockSpec((tm, k), lambda i, j: (i, 0)),
        pl.BlockSpec((1, k), lambda i, j: (0, 0)),
        pl.BlockSpec((k, tn), lambda i, j: (0, j)),
    ]
    args = [x, g.reshape(1, k), w]
    if rope_tab is None:
        body = _norm_matmul_gelu_kernel
    else:
        assert n % (3 * tn) == 0 and tn % V7X_LANES == 0
        section_tiles = n // (3 * tn)
        s_tiles = seq // tm
        in_specs.append(pl.BlockSpec(
            (None, 3, tm, V7X_LANES), lambda i, j: (j // section_tiles, 0, i % s_tiles, 0)))
        args.append(rope_tab)
        body = functools.partial(_norm_matmul_rope_kernel, shift=rot_half)
    return pl.pallas_call(
        body,
        grid=grid,
        in_specs=in_specs,
        out_specs=pl.BlockSpec((tm, tn), lambda i, j: (i, j)),
        out_shape=jax.ShapeDtypeStruct((t, n), BF16),
        scratch_shapes=[pltpu.VMEM((tm, k), BF16)],
        compiler_params=_params(("parallel", "arbitrary")),
        name="norm_matmul_gelu" if rope_tab is None else "norm_matmul_rope",
    )(*args)


def _matmul_residual_kernel(a_ref, w_ref, r_ref, o_ref):
    o_ref[...] = r_ref[...] + jnp.dot(a_ref[...], w_ref[...], preferred_element_type=F32)


def _matmul_residual(a, w, res):
    t, k = a.shape
    n = w.shape[1]
    tm, tn = _tile(t, 512), _tile(n, 2048)
    return pl.pallas_call(
        _matmul_residual_kernel,
        grid=(t // tm, n // tn),
        in_specs=[
            pl.BlockSpec((tm, k), lambda i, j: (i, 0)),
            pl.BlockSpec((k, tn), lambda i, j: (0, j)),
            pl.BlockSpec((tm, tn), lambda i, j: (i, j)),
        ],
        out_specs=pl.BlockSpec((tm, tn), lambda i, j: (i, j)),
        out_shape=jax.ShapeDtypeStruct((t, n), F32),
        compiler_params=_params(("parallel", "parallel")),
        name="matmul_residual",
    )(a, w, res)


def _gmlp_gate_kernel(u_ref, v_ref, lng_ref, lnb_ref, ws_ref, bst_ref, y_ref, vn_ref):
    n_groups, chunk, _ = ws_ref.shape
    tm, width = u_ref.shape
    gd = width // n_groups
    v = v_ref[...].astype(F32)
    vc = v - jnp.mean(v, axis=-1, keepdims=True)
    inv = lax.rsqrt(jnp.mean(vc * vc, axis=-1, keepdims=True) + EPS)
    vn_ref[...] = (vc * inv * lng_ref[...] + lnb_ref[...]).astype(BF16)
    causal = (lax.broadcasted_iota(I32, (chunk, chunk), 0) >= lax.broadcasted_iota(I32, (chunk, chunk), 1))
    for g in range(n_groups):
        w = jnp.where(causal, ws_ref[g], 0.0).astype(BF16)
        bias = bst_ref[:, g:g + 1]
        cols = slice(g * gd, (g + 1) * gd)
        for ci in range(tm // chunk):
            rows = slice(ci * chunk, (ci + 1) * chunk)
            sv = jnp.dot(w, vn_ref[rows, cols], preferred_element_type=F32) + bias
            y_ref[rows, cols] = (u_ref[rows, cols].astype(F32) * sv).astype(y_ref.dtype)


def _gmlp_gate(z, ln_g, ln_b, w_s, b_s):
    t, w2 = z.shape
    width = w2 // 2
    n_groups, chunk, _ = w_s.shape
    tm = _tile(t, 4 * chunk)
    return pl.pallas_call(
        _gmlp_gate_kernel,
        grid=(t // tm,),
        in_specs=[
            pl.BlockSpec((tm, width), lambda i: (i, 0)),
            pl.BlockSpec((tm, width), lambda i: (i, 1)),
            pl.BlockSpec((1, width), lambda i: (0, 0)),
            pl.BlockSpec((1, width), lambda i: (0, 0)),
            pl.BlockSpec((n_groups, chunk, chunk), lambda i: (0, 0, 0)),
            pl.BlockSpec((chunk, n_groups), lambda i: (0, 0)),
        ],
        out_specs=pl.BlockSpec((tm, width), lambda i: (i, 0)),
        out_shape=jax.ShapeDtypeStruct((t, width), BF16),
        scratch_shapes=[pltpu.VMEM((tm, width), BF16)],
        compiler_params=_params(("parallel",)),
        name="gmlp_gate",
    )(z, z, ln_g.reshape(1, width), ln_b.reshape(1, width), w_s, b_s.T)


def _diff_attn_kernel(lam_ref, q_ref, k_ref, v_ref, g_ref, o_ref, qs_ref, s_ref, m_ref, acc_ref, *, out_scale):
    qi = pl.program_id(2)
    n_hd, _, dh2 = qs_ref.shape
    tq = q_ref.shape[0]
    half = dh2 // 2
    row_chunk = _tile(tq, ATTN_ROW_CHUNK)

    def head_cols(hd):
        return slice(hd * dh2, (hd + 1) * dh2)

    for hd in range(n_hd):
        q = q_ref[:, head_cols(hd)]
        lane = lax.broadcasted_iota(I32, q.shape, 1)
        qs_ref[hd, :tq, :] = jnp.where(lane < half, q, jnp.zeros_like(q))
        qs_ref[hd, tq:, :] = jnp.where(lane >= half, q, jnp.zeros_like(q))
    m_ref[...] = jnp.full_like(m_ref, -jnp.inf)
    acc_ref[...] = jnp.zeros_like(acc_ref)

    def kv_rows(kj):
        return pl.ds(pl.multiple_of(kj * tq, tq), tq)

    def scores(hd, kj, slot):
        s_ref[hd, slot] = lax.dot_general(qs_ref[hd], k_ref[kv_rows(kj), head_cols(hd)],
                                          (((1,), (1,)), ((), ())), preferred_element_type=F32)

    def softmax_pv(hd, kj, slot, masked):
        v = v_ref[kv_rows(kj), head_cols(hd)]
        v1 = jnp.concatenate([v, jnp.ones_like(v)], axis=1)
        for r in range(2 * tq // row_chunk):
            rs = slice(r * row_chunk, (r + 1) * row_chunk)
            s = s_ref[hd, slot, rs, :]
            if masked:
                row = lax.broadcasted_iota(I32, s.shape, 0) + (r * row_chunk) % tq
                col = lax.broadcasted_iota(I32, s.shape, 1)
                s = jnp.where(col <= row, s, -jnp.inf)
            m_old = m_ref[hd, rs, :]
            m_new = jnp.maximum(m_old, jnp.max(s, axis=-1, keepdims=True))
            p = jnp.exp((s - m_new).astype(BF16))
            alpha = jnp.exp(m_old - m_new)
            acc_ref[hd, rs, :] = alpha * acc_ref[hd, rs, :] + jnp.dot(p, v1, preferred_element_type=F32)
            m_ref[hd, rs, :] = m_new

    for hd in range(n_hd):
        scores(hd, 0, 0)

    def off_diagonal(kj, carry):
        slot = kj % 2
        for hd in range(n_hd):
            softmax_pv(hd, kj, slot, masked=False)
            scores(hd, kj + 1, 1 - slot)
        return carry

    lax.fori_loop(0, qi, off_diagonal, 0)
    for hd in range(n_hd):
        softmax_pv(hd, qi, qi % 2, masked=True)
        o = acc_ref[hd, :, :dh2] / acc_ref[hd, :, dh2:]
        o = o[:tq] - lam_ref[0] * o[tq:]
        o = o * _rms_scale(o) * g_ref[...] * out_scale
        o_ref[:, head_cols(hd)] = o.astype(o_ref.dtype)


def _diff_attention(qkv, lam, subln_g, *, batch, seq, n_heads, out_scale):
    t = qkv.shape[0]
    dh2 = qkv.shape[1] // (3 * n_heads)
    tq = _tile(seq, 512)
    nq = seq // tq
    n_hd = _tile(n_heads, ATTN_HEADS_PER_STEP)
    n_groups = n_heads // n_hd
    wide = n_hd * dh2
    return pl.pallas_call(
        functools.partial(_diff_attn_kernel, out_scale=out_scale),
        grid=(batch, n_groups, nq),
        in_specs=[
            pl.BlockSpec(memory_space=pltpu.SMEM),
            pl.BlockSpec((tq, wide), lambda b, h, i: (b * nq + i, h)),
            pl.BlockSpec((seq, wide), lambda b, h, i: (b, n_groups + h)),
            pl.BlockSpec((seq, wide), lambda b, h, i: (b, 2 * n_groups + h)),
            pl.BlockSpec((1, dh2), lambda b, h, i: (0, 0)),
        ],
        out_specs=pl.BlockSpec((tq, wide), lambda b, h, i: (b * nq + i, h)),
        out_shape=jax.ShapeDtypeStruct((t, n_heads * dh2), BF16),
        scratch_shapes=[
            pltpu.VMEM((n_hd, 2 * tq, dh2), BF16),
            pltpu.VMEM((n_hd, 2, 2 * tq, tq), F32),
            pltpu.VMEM((n_hd, 2 * tq, 1), F32),
            pltpu.VMEM((n_hd, 2 * tq, 2 * dh2), F32),
        ],
        compiler_params=_params(("parallel", "parallel", "arbitrary")),
        name="diff_attention",
    )(lam.reshape(1), qkv, qkv, qkv, subln_g.reshape(1, dh2))


def _router_kernel(x_ref, g_ref, rw_ref, rb_ref, hp_ref, r_ref, cnt_ref, carry_ref):
    @pl.when(pl.program_id(0) == 0)
    def _():
        carry_ref[...] = jnp.zeros_like(carry_ref)

    x = x_ref[...]
    tm, d = x.shape
    n_exp = rw_ref.shape[1]
    h = x * _rms_scale(x) * g_ref[...]
    h_hi = h.astype(BF16)
    h_hi32 = h_hi.astype(F32)
    half = d // 2
    lo_word = pltpu.bitcast(h_hi32[:, :half], U32) >> 16
    hi_word = pltpu.bitcast(h_hi32[:, half:], U32) & jnp.uint32(HI16)
    hp_ref[...] = lo_word | hi_word

    h_lo = (h - h_hi32).astype(BF16)
    rw = rw_ref[...]
    w_hi = rw.astype(BF16)
    w_lo = (rw - w_hi.astype(F32)).astype(BF16)
    logits = (jnp.dot(h_hi, w_hi, preferred_element_type=F32)
              + jnp.dot(h_lo, w_hi, preferred_element_type=F32)
              + jnp.dot(h_hi, w_lo, preferred_element_type=F32)) + rb_ref[...]

    e_iota = lax.broadcasted_iota(I32, (tm, n_exp), 1).astype(F32)
    work = logits
    vals, onehots, idxs = [], [], []
    for _ in range(TOP_K):
        m = jnp.max(work, axis=-1, keepdims=True)
        idx = jnp.min(jnp.where(work == m, e_iota, float(n_exp)), axis=-1, keepdims=True)
        hit = e_iota == idx
        vals.append(m)
        idxs.append(idx)
        onehots.append(hit)
        work = jnp.where(hit, -jnp.inf, work)
    exps = [jnp.exp(v - vals[0]) for v in vals]
    denom = functools.reduce(lambda a, b: a + b, exps)

    chosen = functools.reduce(jnp.logical_or, onehots)
    chosen_f = jnp.where(chosen, 1.0, 0.0)
    earlier = (lax.broadcasted_iota(I32, (tm, tm), 0) > lax.broadcasted_iota(I32, (tm, tm), 1))
    before = jnp.dot(jnp.where(earlier, 1.0, 0.0).astype(BF16), chosen_f.astype(BF16),
                     preferred_element_type=F32) + carry_ref[...]
    carry_ref[...] = carry_ref[...] + jnp.sum(chosen_f, axis=0, keepdims=True)
    cnt_ref[...] = carry_ref[...]

    lane = lax.broadcasted_iota(I32, r_ref.shape, 1)
    out = jnp.zeros(r_ref.shape, F32)
    for k in range(TOP_K):
        rank = jnp.sum(jnp.where(onehots[k], before, 0.0), axis=-1, keepdims=True)
        out = jnp.where(lane == k, idxs[k], out)
        out = jnp.where(lane == TOP_K + k, rank, out)
        out = jnp.where(lane == 2 * TOP_K + k, exps[k] / denom, out)
    r_ref[...] = out


def _router(x, g, router_w, router_b):
    t, d = x.shape
    n_exp = router_w.shape[1]
    tm = _tile(t, 512)
    return pl.pallas_call(
        _router_kernel,
        grid=(t // tm,),
        in_specs=[
            pl.BlockSpec((tm, d), lambda i: (i, 0)),
            pl.BlockSpec((1, d), lambda i: (0, 0)),
            pl.BlockSpec((d, n_exp), lambda i: (0, 0)),
            pl.BlockSpec((1, n_exp), lambda i: (0, 0)),
        ],
        out_specs=[
            pl.BlockSpec((tm, d // 2), lambda i: (i, 0)),
            pl.BlockSpec((tm, V7X_LANES), lambda i: (i, 0)),
            pl.BlockSpec((1, n_exp), lambda i: (0, 0)),
        ],
        out_shape=[
            jax.ShapeDtypeStruct((t, d // 2), U32),
            jax.ShapeDtypeStruct((t, V7X_LANES), F32),
            jax.ShapeDtypeStruct((1, n_exp), F32),
        ],
        scratch_shapes=[pltpu.VMEM((1, n_exp), F32)],
        compiler_params=_params(("arbitrary",)),
        name="moe_router",
    )(x, g.reshape(1, d), router_w, router_b.reshape(1, n_exp))


def _dispatch_kernel(dest_ref, hp_ref, xs_init_ref, xs_ref, sem):
    del xs_init_ref
    tm = hp_ref.shape[0]

    def row_copy(r, k):
        return pltpu.make_async_copy(hp_ref.at[pl.ds(r, 1), :],
                                     xs_ref.at[pl.ds(dest_ref[r * TOP_K + k], 1), :], sem)

    def issue(r, carry):
        for k in range(TOP_K):
            row_copy(r, k).start(priority=k % 2)
        return carry

    def drain(r, carry):
        for k in range(TOP_K):
            row_copy(r, k).wait()
        return carry

    lax.fori_loop(0, tm, issue, 0)
    lax.fori_loop(0, tm, drain, 0)


def _dispatch(hp, dest_flat, n_slots):
    t, dw = hp.shape
    tm = _tile(t, 256)
    xs_init = jnp.zeros((n_slots, dw), U32)
    return pl.pallas_call(
        _dispatch_kernel,
        grid=(t // tm,),
        in_specs=[
            pl.BlockSpec((tm * TOP_K,), lambda i: (i,), memory_space=pltpu.SMEM),
            pl.BlockSpec((tm, dw), lambda i: (i, 0)),
            pl.BlockSpec(memory_space=pl.ANY),
        ],
        out_specs=pl.BlockSpec(memory_space=pl.ANY),
        out_shape=jax.ShapeDtypeStruct((n_slots, dw), U32),
        scratch_shapes=[pltpu.SemaphoreType.DMA(())],
        input_output_aliases={2: 0},
        compiler_params=_params(("arbitrary",)),
        name="moe_dispatch",
    )(dest_flat, hp, xs_init)


def _experts_kernel(te_ref, ns_ref, nu_ref, xs_ref, wgu_ref, wd_ref, bgu_ref, bd_ref, y_ref, x_ref, *, sub_rows):
    del te_ref, nu_ref
    i, c = pl.program_id(0), pl.program_id(1)
    n_sub = ns_ref[i]
    bm = y_ref.shape[0]
    half_words = xs_ref.shape[1]
    cw = wgu_ref.shape[1]
    hw = cw // 2

    @pl.when(c == 0)
    def _():
        words = xs_ref[...]
        x_ref[:, :half_words] = pltpu.bitcast(words << 16, F32).astype(BF16)
        x_ref[:, half_words:] = pltpu.bitcast(words & jnp.uint32(HI16), F32).astype(BF16)
        for sb in range(bm // sub_rows):
            y_ref[sb * sub_rows:(sb + 1) * sub_rows, :] = jnp.broadcast_to(
                jnp.where(sb < n_sub, bd_ref[...], 0.0), (sub_rows, y_ref.shape[1]))

    def token_rows(m_rows):
        wgu = wgu_ref[...].astype(BF16)
        lo = pltpu.bitcast(wd_ref[:hw // 2, :].astype(BF16).astype(F32), U32) >> 16
        hi = pltpu.bitcast(wd_ref[hw // 2:, :].astype(BF16).astype(F32), U32) & jnp.uint32(HI16)
        wdi = pltpu.bitcast(lo | hi, BF16)
        rows = slice(0, m_rows)
        even = (lax.broadcasted_iota(I32, (m_rows, V7X_LANES), 1) & 1) == 0
        gu = jnp.dot(x_ref[rows, :], wgu, preferred_element_type=F32) + bgu_ref[...]
        gs = jnp.minimum(gu, SWIGLU_LIMIT)
        gate = gs * jax.nn.sigmoid(SWIGLU_ALPHA * gs)
        up = jnp.clip(gu, -SWIGLU_LIMIT, SWIGLU_LIMIT) + 1.0
        prod = []
        for v in range(cw // V7X_LANES):
            cols = slice(v * V7X_LANES, (v + 1) * V7X_LANES)
            prod.append(gate[:, cols] * pltpu.roll(up[:, cols], V7X_LANES - 1, 1))
        nv = hw // V7X_LANES
        act = jnp.concatenate(
            [jnp.where(even, prod[v], pltpu.roll(prod[nv + v], 1, 1)) for v in range(nv)], axis=1)
        y_ref[rows, :] += jnp.dot(act.astype(BF16), wdi, preferred_element_type=F32)

    for k in range(1, bm // sub_rows + 1):
        @pl.when(n_sub == k)
        def _(k=k):
            token_rows(k * sub_rows)


def _experts(xs, tile_expert, tile_subs, n_used, w_gu, w_d, b_gu, b_d, *, layer, bm, sub_rows):
    n_slots, dw = xs.shape
    depth, n_exp, d, f2 = w_gu.shape
    cw = _tile(f2, MOE_GU_CHUNK)
    assert cw % (2 * V7X_LANES) == 0
    nt, nc = n_slots // bm, f2 // cw

    def tile_idx(i, nu):
        return jnp.minimum(i, nu[0] - 1)

    def chunk_idx(i, c, nu):
        return jnp.where(i < nu[0], c, nc - 1)

    def expert(i, te, nu):
        return te[tile_idx(i, nu)]

    grid_spec = pltpu.PrefetchScalarGridSpec(
        num_scalar_prefetch=3,
        grid=(nt, nc),
        in_specs=[
            pl.BlockSpec((bm, dw), lambda i, c, te, ns, nu: (tile_idx(i, nu), 0)),
            pl.BlockSpec((None, None, d, cw),
                         lambda i, c, te, ns, nu: (layer, expert(i, te, nu), 0, chunk_idx(i, c, nu))),
            pl.BlockSpec((None, None, cw // 2, d),
                         lambda i, c, te, ns, nu: (layer, expert(i, te, nu), chunk_idx(i, c, nu), 0)),
            pl.BlockSpec((None, None, 1, cw),
                         lambda i, c, te, ns, nu: (layer, expert(i, te, nu), 0, chunk_idx(i, c, nu))),
            pl.BlockSpec((None, None, 1, d), lambda i, c, te, ns, nu: (layer, expert(i, te, nu), 0, 0)),
        ],
        out_specs=pl.BlockSpec((bm, d), lambda i, c, te, ns, nu: (i, 0)),
        scratch_shapes=[pltpu.VMEM((bm, d), BF16)],
    )
    return pl.pallas_call(
        functools.partial(_experts_kernel, sub_rows=sub_rows),
        grid_spec=grid_spec,
        out_shape=jax.ShapeDtypeStruct((n_slots, d), F32),
        compiler_params=_params(("arbitrary", "arbitrary")),
        name="moe_experts",
    )(tile_expert, tile_subs, n_used, xs, w_gu, w_d,
      b_gu.reshape(depth, n_exp, 1, f2), b_d.reshape(depth, n_exp, 1, d))


def _combine_kernel(dest_ref, x_ref, r_ref, y_ref, *rest, final_norm):
    if final_norm:
        g_ref, o_ref, buf_ref, sem = rest
    else:
        o_ref, buf_ref, sem = rest
    tm = x_ref.shape[0]

    def row_copy(r, k):
        return pltpu.make_async_copy(y_ref.at[pl.ds(dest_ref[r * TOP_K + k], 1), :],
                                     buf_ref.at[k, pl.ds(r, 1), :], sem)

    def issue(r, carry):
        for k in range(TOP_K):
            row_copy(r, k).start(priority=k % 2)
        return carry

    def drain(r, carry):
        for k in range(TOP_K):
            row_copy(r, k).wait()
        return carry

    lax.fori_loop(0, tm, issue, 0)
    lax.fori_loop(0, tm, drain, 0)

    acc = x_ref[...]
    for k in range(TOP_K):
        acc = acc + r_ref[:, 2 * TOP_K + k:2 * TOP_K + k + 1] * buf_ref[k]
    if final_norm:
        acc = acc * _rms_scale(acc) * g_ref[...]
    o_ref[...] = acc


def _combine(x, r, y, dest_flat, final_g=None):
    t, d = x.shape
    tm = _tile(t, 256)
    in_specs = [
        pl.BlockSpec((tm * TOP_K,), lambda i: (i,), memory_space=pltpu.SMEM),
        pl.BlockSpec((tm, d), lambda i: (i, 0)),
        pl.BlockSpec((tm, V7X_LANES), lambda i: (i, 0)),
        pl.BlockSpec(memory_space=pl.ANY),
    ]
    args = [dest_flat, x, r, y]
    if final_g is not None:
        in_specs.append(pl.BlockSpec((1, d), lambda i: (0, 0)))
        args.append(final_g.reshape(1, d))
    return pl.pallas_call(
        functools.partial(_combine_kernel, final_norm=final_g is not None),
        grid=(t // tm,),
        in_specs=in_specs,
        out_specs=pl.BlockSpec((tm, d), lambda i: (i, 0)),
        out_shape=jax.ShapeDtypeStruct((t, d), F32),
        scratch_shapes=[pltpu.VMEM((TOP_K, tm, d), F32), pltpu.SemaphoreType.DMA(())],
        compiler_params=_params(("arbitrary",)),
        name="moe_combine",
    )(*args)


def _moe(x, norm_g, router_w, router_b, w_gate_up, b_gate_up, w_down, b_down, *, layer, final_g=None):
    t, d = x.shape
    n_exp = router_w.shape[1]
    bm = _tile(t, MOE_ROW_TILE)
    sub_rows = _tile(bm, MOE_SUB_ROWS)
    n_tiles = t * TOP_K // bm + n_exp
    hp, r, cnt = _router(x, norm_g, router_w, router_b)

    top_i = r[:, :TOP_K].astype(I32)
    rank = r[:, TOP_K:2 * TOP_K].astype(I32)
    counts = cnt[0].astype(I32)
    exp_tiles = (counts + bm - 1) // bm
    tile_end = jnp.cumsum(exp_tiles)
    tile_start = tile_end - exp_tiles
    dest_flat = (tile_start[top_i] * bm + rank).reshape(-1)
    n_used = tile_end[-1:].astype(I32)
    tile_ids = jnp.arange(n_tiles, dtype=I32)
    tile_expert = jnp.minimum(jnp.sum(tile_end[None, :] <= tile_ids[:, None], axis=1), n_exp - 1).astype(I32)
    tile_rows = jnp.clip(counts[tile_expert] - (tile_ids - tile_start[tile_expert]) * bm, 0, bm)
    tile_subs = jnp.where(tile_ids < n_used[0], (tile_rows + sub_rows - 1) // sub_rows, 0).astype(I32)

    xs = _dispatch(hp, dest_flat, n_tiles * bm)
    y = _experts(xs, tile_expert, tile_subs, n_used, w_gate_up, w_down, b_gate_up, b_down,
                 layer=layer, bm=bm, sub_rows=sub_rows)
    return _combine(x, r, y, dest_flat, final_g)


def _rope_tables(seq, dh, rot_dim, q_scale):
    half = rot_dim // 2
    inv_freq = ROPE_THETA ** (-jnp.arange(0, rot_dim, 2, dtype=F32) / rot_dim)
    ang = jnp.arange(seq, dtype=F32)[:, None] * inv_freq[None, :]
    cos, sin = jnp.cos(ang), jnp.sin(ang)
    ones = jnp.ones((seq, dh - rot_dim), F32)
    zeros_rest = jnp.zeros((seq, dh - rot_dim), F32)
    zeros_half = jnp.zeros((seq, half), F32)
    c = jnp.concatenate([cos, cos, ones], axis=1)
    s1 = jnp.concatenate([zeros_half, sin, zeros_rest], axis=1)
    s2 = jnp.concatenate([-sin, zeros_half, zeros_rest], axis=1)
    k_tab = jnp.stack([jnp.tile(a, (1, 2)) for a in (c, s1, s2)])
    v_tab = jnp.stack([jnp.ones_like(k_tab[0]), jnp.zeros_like(k_tab[0]), jnp.zeros_like(k_tab[0])])
    return jnp.stack([k_tab * q_scale, k_tab, v_tab])


def _gmlp_layer(x, norm_g, w_in, ln_g, ln_b, w_s, b_s, w_out):
    z = _norm_matmul(x, norm_g, w_in.astype(BF16))
    y = _gmlp_gate(z, ln_g, ln_b, w_s, b_s)
    return _matmul_residual(y, w_out.astype(BF16), x)


def _attn_layer(x, norm_g, w_qkv, lq1, lk1, lq2, lk2, subln_g, w_o, lambda_init, *, batch, seq):
    d = x.shape[1]
    dh = lq1.shape[0]
    n_heads = d // (2 * dh)
    rot_dim = dh // 4
    tab = _rope_tables(seq, dh, rot_dim, dh ** -0.5)
    qkv = _norm_matmul(x, norm_g, w_qkv.astype(BF16), rope_tab=tab, seq=seq, rot_half=rot_dim // 2)
    lam = jnp.exp(jnp.sum(lq1 * lk1)) - jnp.exp(jnp.sum(lq2 * lk2)) + lambda_init
    o = _diff_attention(qkv, lam.astype(F32), subln_g, batch=batch, seq=seq, n_heads=n_heads,
                        out_scale=1.0 - lambda_init)
    return _matmul_residual(o, w_o.astype(BF16), x)


def kernel(x, attn_norm_g, ffn_norm_g, final_norm_g, a_w_in, a_ln_g, a_ln_b, a_w_s, a_b_s, a_w_out,
           b_w_qkv, b_lambda_q1, b_lambda_k1, b_lambda_q2, b_lambda_k2, b_subln_g, b_w_o,
           router_w, router_b, w_gate_up, b_gate_up, w_down, b_down):
    batch, seq, d = x.shape
    depth = attn_norm_g.shape[0]
    n_mixers = 2
    h = x.reshape(batch * seq, d)
    for i in range(depth):
        j = i // n_mixers
        if i % n_mixers == 0:
            h = _gmlp_layer(h, attn_norm_g[i], a_w_in[j], a_ln_g[j], a_ln_b[j], a_w_s[j], a_b_s[j], a_w_out[j])
        else:
            lambda_init = 0.8 - 0.6 * math.exp(-0.3 * i)
            h = _attn_layer(h, attn_norm_g[i], b_w_qkv[j], b_lambda_q1[j], b_lambda_k1[j], b_lambda_q2[j],
                            b_lambda_k2[j], b_subln_g[j], b_w_o[j], lambda_init, batch=batch, seq=seq)
        h = _moe(h, ffn_norm_g[i], router_w[i], router_b[i], w_gate_up, b_gate_up, w_down, b_down,
                 layer=i, final_g=final_norm_g if i == depth - 1 else None)
    return h.reshape(batch, seq, d)
```

```python
import functools
import math

import jax
import jax.numpy as jnp
from jax import lax
from jax.experimental import pallas as pl
from jax.experimental.pallas import tpu as pltpu

F32 = jnp.float32
BF16 = jnp.bfloat16
I32 = jnp.int32
U32 = jnp.uint32

EPS = 1e-5
TOP_K = 4
ROPE_THETA = 500000.0
SWIGLU_LIMIT = 7.0
SWIGLU_ALPHA = 1.702

V7X_LANES = 128
V7X_VMEM_BYTES = 64 * 1024 * 1024
VMEM_LIMIT_BYTES = V7X_VMEM_BYTES * 7 // 8
HI16 = 0xFFFF0000

ATTN_HEADS_PER_STEP = 1
ATTN_ROW_CHUNK = 512
COMBINE_PARTS = 2
MOE_ROW_TILE = 768
MOE_SUB_ROWS = 128
MOE_GU_CHUNK = 1024


def _tile(dim, want):
    t = min(dim, want)
    while dim % t:
        t //= 2
    return t


def _params(semantics):
    return pltpu.CompilerParams(dimension_semantics=semantics, vmem_limit_bytes=VMEM_LIMIT_BYTES)


def _rms_scale(x):
    return lax.rsqrt(jnp.mean(x * x, axis=-1, keepdims=True) + EPS)


def _norm_matmul_gelu_kernel(x_ref, g_ref, w_ref, o_ref, xn_ref):
    @pl.when(pl.program_id(1) == 0)
    def _():
        x = x_ref[...]
        xn_ref[...] = (x * _rms_scale(x) * g_ref[...]).astype(BF16)

    y = jnp.dot(xn_ref[...], w_ref[...], preferred_element_type=F32)
    o_ref[...] = jax.nn.gelu(y).astype(o_ref.dtype)


def _norm_matmul_rope_kernel(x_ref, g_ref, w_ref, tab_ref, o_ref, xn_ref, *, shift):
    @pl.when(pl.program_id(1) == 0)
    def _():
        x = x_ref[...]
        xn_ref[...] = (x * _rms_scale(x) * g_ref[...]).astype(BF16)

    y = jnp.dot(xn_ref[...], w_ref[...], preferred_element_type=F32)
    c, s1, s2 = tab_ref[0], tab_ref[1], tab_ref[2]
    for h in range(y.shape[1] // V7X_LANES):
        cols = slice(h * V7X_LANES, (h + 1) * V7X_LANES)
        yh = y[:, cols]
        r = (yh * c + pltpu.roll(yh, shift, 1) * s1
             + pltpu.roll(yh, V7X_LANES - shift, 1) * s2)
        o_ref[:, cols] = r.astype(o_ref.dtype)


def _norm_matmul(x, g, w, *, rope_tab=None, seq=None, rot_half=0):
    t, k = x.shape
    n = w.shape[1]
    tm = _tile(t if seq is None else seq, 1024)
    tn = _tile(n if rope_tab is None else n // 3, 1024)
    grid = (t // tm, n // tn)
    in_specs = [
        pl.BlockSpec((tm, k), lambda i, j: (i, 0)),
        pl.BlockSpec((1, k), lambda i, j: (0, 0)),
        pl.BlockSpec((k, tn), lambda i, j: (0, j)),
    ]
    args = [x, g.reshape(1, k), w]
    if rope_tab is None:
        body = _norm_matmul_gelu_kernel
    else:
        assert n % (3 * tn) == 0 and tn % V7X_LANES == 0
        section_tiles = n // (3 * tn)
        s_tiles = seq // tm
        in_specs.append(pl.BlockSpec(
            (None, 3, tm, V7X_LANES), lambda i, j: (j // section_tiles, 0, i % s_tiles, 0)))
        args.append(rope_tab)
        body = functools.partial(_norm_matmul_rope_kernel, shift=rot_half)
    return pl.pallas_call(
        body,
        grid=grid,
        in_specs=in_specs,
        out_specs=pl.BlockSpec((tm, tn), lambda i, j: (i, j)),
        out_shape=jax.ShapeDtypeStruct((t, n), BF16),
        scratch_shapes=[pltpu.VMEM((tm, k), BF16)],
        compiler_params=_params(("parallel", "arbitrary")),
        name="norm_matmul_gelu" if rope_tab is None else "norm_matmul_rope",
    )(*args)


def _matmul_residual_kernel(a_ref, w_ref, r_ref, o_ref):
    o_ref[...] = r_ref[...] + jnp.dot(a_ref[...], w_ref[...], preferred_element_type=F32)


def _matmul_residual(a, w, res):
    t, k = a.shape
    n = w.shape[1]
    tm, tn = _tile(t, 512), _tile(n, 2048)
    return pl.pallas_call(
        _matmul_residual_kernel,
        grid=(t // tm, n // tn),
        in_specs=[
            pl.BlockSpec((tm, k), lambda i, j: (i, 0)),
            pl.BlockSpec((k, tn), lambda i, j: (0, j)),
            pl.BlockSpec((tm, tn), lambda i, j: (i, j)),
        ],
        out_specs=pl.BlockSpec((tm, tn), lambda i, j: (i, j)),
        out_shape=jax.ShapeDtypeStruct((t, n), F32),
        compiler_params=_params(("parallel", "parallel")),
        name="matmul_residual",
    )(a, w, res)


def _gmlp_gate_kernel(u_ref, v_ref, lng_ref, lnb_ref, ws_ref, bst_ref, y_ref, vn_ref):
    n_groups, chunk, _ = ws_ref.shape
    tm, width = u_ref.shape
    gd = width // n_groups
    v = v_ref[...].astype(F32)
    vc = v - jnp.mean(v, axis=-1, keepdims=True)
    inv = lax.rsqrt(jnp.mean(vc * vc, axis=-1, keepdims=True) + EPS)
    vn_ref[...] = (vc * inv * lng_ref[...] + lnb_ref[...]).astype(BF16)
    causal = (lax.broadcasted_iota(I32, (chunk, chunk), 0) >= lax.broadcasted_iota(I32, (chunk, chunk), 1))
    for g in range(n_groups):
        w = jnp.where(causal, ws_ref[g], 0.0).astype(BF16)
        bias = bst_ref[:, g:g + 1]
        cols = slice(g * gd, (g + 1) * gd)
        for ci in range(tm // chunk):
            rows = slice(ci * chunk, (ci + 1) * chunk)
            sv = jnp.dot(w, vn_ref[rows, cols], preferred_element_type=F32) + bias
            y_ref[rows, cols] = (u_ref[rows, cols].astype(F32) * sv).astype(y_ref.dtype)


def _gmlp_gate(z, ln_g, ln_b, w_s, b_s):
    t, w2 = z.shape
    width = w2 // 2
    n_groups, chunk, _ = w_s.shape
    tm = _tile(t, 4 * chunk)
    return pl.pallas_call(
        _gmlp_gate_kernel,
        grid=(t // tm,),
        in_specs=[
            pl.BlockSpec((tm, width), lambda i: (i, 0)),
            pl.BlockSpec((tm, width), lambda i: (i, 1)),
            pl.BlockSpec((1, width), lambda i: (0, 0)),
            pl.BlockSpec((1, width), lambda i: (0, 0)),
            pl.BlockSpec((n_groups, chunk, chunk), lambda i: (0, 0, 0)),
            pl.BlockSpec((chunk, n_groups), lambda i: (0, 0)),
        ],
        out_specs=pl.BlockSpec((tm, width), lambda i: (i, 0)),
        out_shape=jax.ShapeDtypeStruct((t, width), BF16),
        scratch_shapes=[pltpu.VMEM((tm, width), BF16)],
        compiler_params=_params(("parallel",)),
        name="gmlp_gate",
    )(z, z, ln_g.reshape(1, width), ln_b.reshape(1, width), w_s, b_s.T)


def _diff_attn_kernel(lam_ref, q_ref, k_ref, v_ref, g_ref, o_ref, qs_ref, s_ref, m_ref, acc_ref, *, out_scale):
    qi = pl.program_id(2)
    n_hd, _, dh2 = qs_ref.shape
    tq = q_ref.shape[0]
    half = dh2 // 2
    row_chunk = _tile(tq, ATTN_ROW_CHUNK)

    def head_cols(hd):
        return slice(hd * dh2, (hd + 1) * dh2)

    for hd in range(n_hd):
        q = q_ref[:, head_cols(hd)]
        lane = lax.broadcasted_iota(I32, q.shape, 1)
        qs_ref[hd, :tq, :] = jnp.where(lane < half, q, jnp.zeros_like(q))
        qs_ref[hd, tq:, :] = jnp.where(lane >= half, q, jnp.zeros_like(q))
    m_ref[...] = jnp.full_like(m_ref, -jnp.inf)
    acc_ref[...] = jnp.zeros_like(acc_ref)

    def kv_rows(kj):
        return pl.ds(pl.multiple_of(kj * tq, tq), tq)

    def scores(hd, kj, slot):
        s_ref[hd, slot] = lax.dot_general(qs_ref[hd], k_ref[kv_rows(kj), head_cols(hd)],
                                          (((1,), (1,)), ((), ())), preferred_element_type=F32)

    def softmax_pv(hd, kj, slot, masked):
        v = v_ref[kv_rows(kj), head_cols(hd)]
        v1 = jnp.concatenate([v, jnp.ones_like(v)], axis=1)
        for r in range(2 * tq // row_chunk):
            rs = slice(r * row_chunk, (r + 1) * row_chunk)
            s = s_ref[hd, slot, rs, :]
            if masked:
                row = lax.broadcasted_iota(I32, s.shape, 0) + (r * row_chunk) % tq
                col = lax.broadcasted_iota(I32, s.shape, 1)
                s = jnp.where(col <= row, s, -jnp.inf)
            m_old = m_ref[hd, rs, :]
            m_new = jnp.maximum(m_old, jnp.max(s, axis=-1, keepdims=True))
            p = jnp.exp((s - m_new).astype(BF16))
            alpha = jnp.exp(m_old - m_new)
            acc_ref[hd, rs, :] = alpha * acc_ref[hd, rs, :] + jnp.dot(p, v1, preferred_element_type=F32)
            m_ref[hd, rs, :] = m_new

    for hd in range(n_hd):
        scores(hd, 0, 0)

    def off_diagonal(kj, carry):
        slot = kj % 2
        for hd in range(n_hd):
            softmax_pv(hd, kj, slot, masked=False)
            scores(hd, kj + 1, 1 - slot)
        return carry

    lax.fori_loop(0, qi, off_diagonal, 0)
    for hd in range(n_hd):
        softmax_pv(hd, qi, qi % 2, masked=True)
        o = acc_ref[hd, :, :dh2] / acc_ref[hd, :, dh2:]
        o = o[:tq] - lam_ref[0] * o[tq:]
        o = o * _rms_scale(o) * g_ref[...] * out_scale
        o_ref[:, head_cols(hd)] = o.astype(o_ref.dtype)


def _diff_attention(qkv, lam, subln_g, *, batch, seq, n_heads, out_scale):
    t = qkv.shape[0]
    dh2 = qkv.shape[1] // (3 * n_heads)
    tq = _tile(seq, 512)
    nq = seq // tq
    n_hd = _tile(n_heads, ATTN_HEADS_PER_STEP)
    n_groups = n_heads // n_hd
    wide = n_hd * dh2
    return pl.pallas_call(
        functools.partial(_diff_attn_kernel, out_scale=out_scale),
        grid=(batch, n_groups, nq),
        in_specs=[
            pl.BlockSpec(memory_space=pltpu.SMEM),
            pl.BlockSpec((tq, wide), lambda b, h, i: (b * nq + i, h)),
            pl.BlockSpec((seq, wide), lambda b, h, i: (b, n_groups + h)),
            pl.BlockSpec((seq, wide), lambda b, h, i: (b, 2 * n_groups + h)),
            pl.BlockSpec((1, dh2), lambda b, h, i: (0, 0)),
        ],
        out_specs=pl.BlockSpec((tq, wide), lambda b, h, i: (b * nq + i, h)),
        out_shape=jax.ShapeDtypeStruct((t, n_heads * dh2), BF16),
        scratch_shapes=[
            pltpu.VMEM((n_hd, 2 * tq, dh2), BF16),
            pltpu.VMEM((n_hd, 2, 2 * tq, tq), F32),
            pltpu.VMEM((n_hd, 2 * tq, 1), F32),
            pltpu.VMEM((n_hd, 2 * tq, 2 * dh2), F32),
        ],
        compiler_params=_params(("parallel", "parallel", "arbitrary")),
        name="diff_attention",
    )(lam.reshape(1), qkv, qkv, qkv, subln_g.reshape(1, dh2))


def _router_kernel(x_ref, g_ref, rw_ref, rb_ref, hp_ref, r_ref, cnt_ref, carry_ref):
    @pl.when(pl.program_id(0) == 0)
    def _():
        carry_ref[...] = jnp.zeros_like(carry_ref)

    x = x_ref[...]
    tm, d = x.shape
    n_exp = rw_ref.shape[1]
    h = x * _rms_scale(x) * g_ref[...]
    h_hi = h.astype(BF16)
    h_hi32 = h_hi.astype(F32)
    half = d // 2
    lo_word = pltpu.bitcast(h_hi32[:, :half], U32) >> 16
    hi_word = pltpu.bitcast(h_hi32[:, half:], U32) & jnp.uint32(HI16)
    hp_ref[...] = lo_word | hi_word

    h_lo = (h - h_hi32).astype(BF16)
    rw = rw_ref[...]
    w_hi = rw.astype(BF16)
    w_lo = (rw - w_hi.astype(F32)).astype(BF16)
    logits = (jnp.dot(h_hi, w_hi, preferred_element_type=F32)
              + jnp.dot(h_lo, w_hi, preferred_element_type=F32)
              + jnp.dot(h_hi, w_lo, preferred_element_type=F32)) + rb_ref[...]

    e_iota = lax.broadcasted_iota(I32, (tm, n_exp), 1).astype(F32)
    work = logits
    vals, onehots, idxs = [], [], []
    for _ in range(TOP_K):
        m = jnp.max(work, axis=-1, keepdims=True)
        idx = jnp.min(jnp.where(work == m, e_iota, float(n_exp)), axis=-1, keepdims=True)
        hit = e_iota == idx
        vals.append(m)
        idxs.append(idx)
        onehots.append(hit)
        work = jnp.where(hit, -jnp.inf, work)
    exps = [jnp.exp(v - vals[0]) for v in vals]
    denom = functools.reduce(lambda a, b: a + b, exps)

    chosen = functools.reduce(jnp.logical_or, onehots)
    chosen_f = jnp.where(chosen, 1.0, 0.0)
    earlier = (lax.broadcasted_iota(I32, (tm, tm), 0) > lax.broadcasted_iota(I32, (tm, tm), 1))
    before = jnp.dot(jnp.where(earlier, 1.0, 0.0).astype(BF16), chosen_f.astype(BF16),
                     preferred_element_type=F32) + carry_ref[...]
    carry_ref[...] = carry_ref[...] + jnp.sum(chosen_f, axis=0, keepdims=True)
    cnt_ref[...] = carry_ref[...]

    lane = lax.broadcasted_iota(I32, r_ref.shape, 1)
    out = jnp.zeros(r_ref.shape, F32)
    for k in range(TOP_K):
        rank = jnp.sum(jnp.where(onehots[k], before, 0.0), axis=-1, keepdims=True)
        out = jnp.where(lane == k, idxs[k], out)
        out = jnp.where(lane == TOP_K + k, rank, out)
        out = jnp.where(lane == 2 * TOP_K + k, exps[k] / denom, out)
    r_ref[...] = out


def _router(x, g, router_w, router_b):
    t, d = x.shape
    n_exp = router_w.shape[1]
    tm = _tile(t, 512)
    return pl.pallas_call(
        _router_kernel,
        grid=(t // tm,),
        in_specs=[
            pl.BlockSpec((tm, d), lambda i: (i, 0)),
            pl.BlockSpec((1, d), lambda i: (0, 0)),
            pl.BlockSpec((d, n_exp), lambda i: (0, 0)),
            pl.BlockSpec((1, n_exp), lambda i: (0, 0)),
        ],
        out_specs=[
            pl.BlockSpec((tm, d // 2), lambda i: (i, 0)),
            pl.BlockSpec((tm, V7X_LANES), lambda i: (i, 0)),
            pl.BlockSpec((1, n_exp), lambda i: (0, 0)),
        ],
        out_shape=[
            jax.ShapeDtypeStruct((t, d // 2), U32),
            jax.ShapeDtypeStruct((t, V7X_LANES), F32),
            jax.ShapeDtypeStruct((1, n_exp), F32),
        ],
        scratch_shapes=[pltpu.VMEM((1, n_exp), F32)],
        compiler_params=_params(("arbitrary",)),
        name="moe_router",
    )(x, g.reshape(1, d), router_w, router_b.reshape(1, n_exp))


def _dispatch_kernel(dest_ref, hp_ref, xs_init_ref, xs_ref, sem):
    del xs_init_ref
    tm = hp_ref.shape[0]

    def row_copy(r, k):
        return pltpu.make_async_copy(hp_ref.at[pl.ds(r, 1), :],
                                     xs_ref.at[pl.ds(dest_ref[r * TOP_K + k], 1), :], sem)

    def issue(r, carry):
        for k in range(TOP_K):
            row_copy(r, k).start(priority=k % 2)
        return carry

    def drain(r, carry):
        for k in range(TOP_K):
            row_copy(r, k).wait()
        return carry

    lax.fori_loop(0, tm, issue, 0)
    lax.fori_loop(0, tm, drain, 0)


def _dispatch(hp, dest_flat, n_slots):
    t, dw = hp.shape
    tm = _tile(t, 256)
    xs_init = jnp.zeros((n_slots, dw), U32)
    return pl.pallas_call(
        _dispatch_kernel,
        grid=(t // tm,),
        in_specs=[
            pl.BlockSpec((tm * TOP_K,), lambda i: (i,), memory_space=pltpu.SMEM),
            pl.BlockSpec((tm, dw), lambda i: (i, 0)),
            pl.BlockSpec(memory_space=pl.ANY),
        ],
        out_specs=pl.BlockSpec(memory_space=pl.ANY),
        out_shape=jax.ShapeDtypeStruct((n_slots, dw), U32),
        scratch_shapes=[pltpu.SemaphoreType.DMA(())],
        input_output_aliases={2: 0},
        compiler_params=_params(("arbitrary",)),
        name="moe_dispatch",
    )(dest_flat, hp, xs_init)


def _experts_kernel(te_ref, ns_ref, nu_ref, xs_ref, wgu_ref, wd_ref, bgu_ref, bd_ref, y_ref, x_ref, *, sub_rows):
    del te_ref, nu_ref
    i, c = pl.program_id(0), pl.program_id(1)
    n_sub = ns_ref[i]
    bm = y_ref.shape[0]
    half_words = xs_ref.shape[1]
    cw = wgu_ref.shape[1]
    hw = cw // 2

    @pl.when(c == 0)
    def _():
        words = xs_ref[...]
        x_ref[:, :half_words] = pltpu.bitcast(words << 16, F32).astype(BF16)
        x_ref[:, half_words:] = pltpu.bitcast(words & jnp.uint32(HI16), F32).astype(BF16)
        for sb in range(bm // sub_rows):
            y_ref[sb * sub_rows:(sb + 1) * sub_rows, :] = jnp.broadcast_to(
                jnp.where(sb < n_sub, bd_ref[...], 0.0), (sub_rows, y_ref.shape[1]))

    def token_rows(m_rows):
        wgu = wgu_ref[...].astype(BF16)
        lo = pltpu.bitcast(wd_ref[:hw // 2, :].astype(BF16).astype(F32), U32) >> 16
        hi = pltpu.bitcast(wd_ref[hw // 2:, :].astype(BF16).astype(F32), U32) & jnp.uint32(HI16)
        wdi = pltpu.bitcast(lo | hi, BF16)
        rows = slice(0, m_rows)
        even = (lax.broadcasted_iota(I32, (m_rows, V7X_LANES), 1) & 1) == 0
        gu = jnp.dot(x_ref[rows, :], wgu, preferred_element_type=F32) + bgu_ref[...]
        gs = jnp.minimum(gu, SWIGLU_LIMIT)
        gate = gs * jax.nn.sigmoid(SWIGLU_ALPHA * gs)
        up = jnp.clip(gu, -SWIGLU_LIMIT, SWIGLU_LIMIT) + 1.0
        prod = []
        for v in range(cw // V7X_LANES):
            cols = slice(v * V7X_LANES, (v + 1) * V7X_LANES)
            prod.append(gate[:, cols] * pltpu.roll(up[:, cols], V7X_LANES - 1, 1))
        nv = hw // V7X_LANES
        act = jnp.concatenate(
            [jnp.where(even, prod[v], pltpu.roll(prod[nv + v], 1, 1)) for v in range(nv)], axis=1)
        y_ref[rows, :] += jnp.dot(act.astype(BF16), wdi, preferred_element_type=F32)

    for k in range(1, bm // sub_rows + 1):
        @pl.when(n_sub == k)
        def _(k=k):
            token_rows(k * sub_rows)


def _experts(xs, tile_expert, tile_subs, n_used, w_gu, w_d, b_gu, b_d, *, layer, bm, sub_rows):
    n_slots, dw = xs.shape
    depth, n_exp, d, f2 = w_gu.shape
    cw = _tile(f2, MOE_GU_CHUNK)
    assert cw % (2 * V7X_LANES) == 0
    nt, nc = n_slots // bm, f2 // cw

    def tile_idx(i, nu):
        return jnp.minimum(i, nu[0] - 1)

    def chunk_idx(i, c, nu):
        return jnp.where(i < nu[0], c, nc - 1)

    def expert(i, te, nu):
        return te[tile_idx(i, nu)]

    grid_spec = pltpu.PrefetchScalarGridSpec(
        num_scalar_prefetch=3,
        grid=(nt, nc),
        in_specs=[
            pl.BlockSpec((bm, dw), lambda i, c, te, ns, nu: (tile_idx(i, nu), 0)),
            pl.BlockSpec((None, None, d, cw),
                         lambda i, c, te, ns, nu: (layer, expert(i, te, nu), 0, chunk_idx(i, c, nu))),
            pl.BlockSpec((None, None, cw // 2, d),
                         lambda i, c, te, ns, nu: (layer, expert(i, te, nu), chunk_idx(i, c, nu), 0)),
            pl.BlockSpec((None, None, 1, cw),
                         lambda i, c, te, ns, nu: (layer, expert(i, te, nu), 0, chunk_idx(i, c, nu))),
            pl.BlockSpec((None, None, 1, d), lambda i, c, te, ns, nu: (layer, expert(i, te, nu), 0, 0)),
        ],
        out_specs=pl.BlockSpec((bm, d), lambda i, c, te, ns, nu: (i, 0)),
        scratch_shapes=[pltpu.VMEM((bm, d), BF16)],
    )
    return pl.pallas_call(
        functools.partial(_experts_kernel, sub_rows=sub_rows),
        grid_spec=grid_spec,
        out_shape=jax.ShapeDtypeStruct((n_slots, d), F32),
        compiler_params=_params(("arbitrary", "arbitrary")),
        name="moe_experts",
    )(tile_expert, tile_subs, n_used, xs, w_gu, w_d,
      b_gu.reshape(depth, n_exp, 1, f2), b_d.reshape(depth, n_exp, 1, d))


def _combine_kernel(dest_ref, x_ref, r_ref, y_ref, *rest, final_norm):
    if final_norm:
        g_ref, o_ref, buf_ref, sem = rest
    else:
        o_ref, buf_ref, sem = rest
    tm = x_ref.shape[0]
    n_parts = sem.shape[0]
    part = tm // n_parts

    def row_copy(r, k, p):
        return pltpu.make_async_copy(y_ref.at[pl.ds(dest_ref[r * TOP_K + k], 1), :],
                                     buf_ref.at[k, pl.ds(r, 1), :], sem.at[p])

    for p in range(n_parts):
        def issue(r, carry, p=p):
            for k in range(TOP_K):
                row_copy(r, k, p).start(priority=k % 2)
            return carry
        lax.fori_loop(p * part, (p + 1) * part, issue, 0)

    for p in range(n_parts):
        def drain(r, carry, p=p):
            for k in range(TOP_K):
                row_copy(r, k, p).wait()
            return carry
        lax.fori_loop(p * part, (p + 1) * part, drain, 0)

        rows = slice(p * part, (p + 1) * part)
        acc = x_ref[rows, :]
        for k in range(TOP_K):
            acc = acc + r_ref[rows, 2 * TOP_K + k:2 * TOP_K + k + 1] * buf_ref[k, rows, :]
        if final_norm:
            acc = acc * _rms_scale(acc) * g_ref[...]
        o_ref[rows, :] = acc


def _combine(x, r, y, dest_flat, final_g=None):
    t, d = x.shape
    tm = _tile(t, 256)
    in_specs = [
        pl.BlockSpec((tm * TOP_K,), lambda i: (i,), memory_space=pltpu.SMEM),
        pl.BlockSpec((tm, d), lambda i: (i, 0)),
        pl.BlockSpec((tm, V7X_LANES), lambda i: (i, 0)),
        pl.BlockSpec(memory_space=pl.ANY),
    ]
    args = [dest_flat, x, r, y]
    if final_g is not None:
        in_specs.append(pl.BlockSpec((1, d), lambda i: (0, 0)))
        args.append(final_g.reshape(1, d))
    return pl.pallas_call(
        functools.partial(_combine_kernel, final_norm=final_g is not None),
        grid=(t // tm,),
        in_specs=in_specs,
        out_specs=pl.BlockSpec((tm, d), lambda i: (i, 0)),
        out_shape=jax.ShapeDtypeStruct((t, d), F32),
        scratch_shapes=[pltpu.VMEM((TOP_K, tm, d), F32), pltpu.SemaphoreType.DMA((COMBINE_PARTS,))],
        compiler_params=_params(("arbitrary",)),
        name="moe_combine",
    )(*args)


def _moe(x, norm_g, router_w, router_b, w_gate_up, b_gate_up, w_down, b_down, *, layer, final_g=None):
    t, d = x.shape
    n_exp = router_w.shape[1]
    bm = MOE_ROW_TILE
    sub_rows = _tile(bm, MOE_SUB_ROWS)
    n_tiles = t * TOP_K // bm + n_exp
    hp, r, cnt = _router(x, norm_g, router_w, router_b)

    top_i = r[:, :TOP_K].astype(I32)
    rank = r[:, TOP_K:2 * TOP_K].astype(I32)
    counts = cnt[0].astype(I32)
    exp_tiles = (counts + bm - 1) // bm
    tile_end = jnp.cumsum(exp_tiles)
    tile_start = tile_end - exp_tiles
    dest_flat = (tile_start[top_i] * bm + rank).reshape(-1)
    n_used = tile_end[-1:].astype(I32)
    tile_ids = jnp.arange(n_tiles, dtype=I32)
    tile_expert = jnp.minimum(jnp.sum(tile_end[None, :] <= tile_ids[:, None], axis=1), n_exp - 1).astype(I32)
    tile_rows = jnp.clip(counts[tile_expert] - (tile_ids - tile_start[tile_expert]) * bm, 0, bm)
    tile_subs = jnp.where(tile_ids < n_used[0], (tile_rows + sub_rows - 1) // sub_rows, 0).astype(I32)

    xs = _dispatch(hp, dest_flat, n_tiles * bm)
    y = _experts(xs, tile_expert, tile_subs, n_used, w_gate_up, w_down, b_gate_up, b_down,
                 layer=layer, bm=bm, sub_rows=sub_rows)
    return _combine(x, r, y, dest_flat, final_g)


def _rope_tables(seq, dh, rot_dim, q_scale):
    half = rot_dim // 2
    inv_freq = ROPE_THETA ** (-jnp.arange(0, rot_dim, 2, dtype=F32) / rot_dim)
    ang = jnp.arange(seq, dtype=F32)[:, None] * inv_freq[None, :]
    cos, sin = jnp.cos(ang), jnp.sin(ang)
    ones = jnp.ones((seq, dh - rot_dim), F32)
    zeros_rest = jnp.zeros((seq, dh - rot_dim), F32)
    zeros_half = jnp.zeros((seq, half), F32)
    c = jnp.concatenate([cos, cos, ones], axis=1)
    s1 = jnp.concatenate([zeros_half, sin, zeros_rest], axis=1)
    s2 = jnp.concatenate([-sin, zeros_half, zeros_rest], axis=1)
    k_tab = jnp.stack([jnp.tile(a, (1, 2)) for a in (c, s1, s2)])
    v_tab = jnp.stack([jnp.ones_like(k_tab[0]), jnp.zeros_like(k_tab[0]), jnp.zeros_like(k_tab[0])])
    return jnp.stack([k_tab * q_scale, k_tab, v_tab])


def _gmlp_layer(x, norm_g, w_in, ln_g, ln_b, w_s, b_s, w_out):
    z = _norm_matmul(x, norm_g, w_in.astype(BF16))
    y = _gmlp_gate(z, ln_g, ln_b, w_s, b_s)
    return _matmul_residual(y, w_out.astype(BF16), x)


def _attn_layer(x, norm_g, w_qkv, lq1, lk1, lq2, lk2, subln_g, w_o, lambda_init, *, batch, seq):
    d = x.shape[1]
    dh = lq1.shape[0]
    n_heads = d // (2 * dh)
    rot_dim = dh // 4
    tab = _rope_tables(seq, dh, rot_dim, dh ** -0.5)
    qkv = _norm_matmul(x, norm_g, w_qkv.astype(BF16), rope_tab=tab, seq=seq, rot_half=rot_dim // 2)
    lam = jnp.exp(jnp.sum(lq1 * lk1)) - jnp.exp(jnp.sum(lq2 * lk2)) + lambda_init
    o = _diff_attention(qkv, lam.astype(F32), subln_g, batch=batch, seq=seq, n_heads=n_heads,
                        out_scale=1.0 - lambda_init)
    return _matmul_residual(o, w_o.astype(BF16), x)


def kernel(x, attn_norm_g, ffn_norm_g, final_norm_g, a_w_in, a_ln_g, a_ln_b, a_w_s, a_b_s, a_w_out,
           b_w_qkv, b_lambda_q1, b_lambda_k1, b_lambda_q2, b_lambda_k2, b_subln_g, b_w_o,
           router_w, router_b, w_gate_up, b_gate_up, w_down, b_down):
    batch, seq, d = x.shape
    depth = attn_norm_g.shape[0]
    n_mixers = 2
    h = x.reshape(batch * seq, d)
    for i in range(depth):
        j = i // n_mixers
        if i % n_mixers == 0:
            h = _gmlp_layer(h, attn_norm_g[i], a_w_in[j], a_ln_g[j], a_ln_b[j], a_w_s[j], a_b_s[j], a_w_out[j])
        else:
            lambda_init = 0.8 - 0.6 * math.exp(-0.3 * i)
            h = _attn_layer(h, attn_norm_g[i], b_w_qkv[j], b_lambda_q1[j], b_lambda_k1[j], b_lambda_q2[j],
                            b_lambda_k2[j], b_subln_g[j], b_w_o[j], lambda_init, batch=batch, seq=seq)
        h = _moe(h, ffn_norm_g[i], router_w[i], router_b[i], w_gate_up, b_gate_up, w_down, b_down,
                 layer=i, final_g=final_norm_g if i == depth - 1 else None)
    return h.reshape(batch, seq, d)
```

```python
import functools
import math

import jax
import jax.numpy as jnp
from jax import lax
from jax.experimental import pallas as pl
from jax.experimental.pallas import tpu as pltpu

F32 = jnp.float32
BF16 = jnp.bfloat16
I32 = jnp.int32
U32 = jnp.uint32

EPS = 1e-5
TOP_K = 4
ROPE_THETA = 500000.0
SWIGLU_LIMIT = 7.0
SWIGLU_ALPHA = 1.702

V7X_LANES = 128
V7X_VMEM_BYTES = 64 * 1024 * 1024
VMEM_LIMIT_BYTES = V7X_VMEM_BYTES * 7 // 8
VMEM_LIMIT_EXPERTS_BYTES = V7X_VMEM_BYTES * 15 // 16
HI16 = 0xFFFF0000

ATTN_Q_TILE = 512
COMBINE_PARTS = 1
MOE_ROW_TILE = 1024
MOE_SUB_ROWS = 128
MOE_GU_CHUNK = 1024


def _tile(dim, want):
    t = min(dim, want)
    while dim % t:
        t //= 2
    return t


def _params(semantics, vmem_limit_bytes=VMEM_LIMIT_BYTES):
    return pltpu.CompilerParams(dimension_semantics=semantics, vmem_limit_bytes=vmem_limit_bytes)


def _rms_scale(x):
    return lax.rsqrt(jnp.mean(x * x, axis=-1, keepdims=True) + EPS)


def _norm_matmul_gelu_kernel(x_ref, g_ref, w_ref, o_ref, xn_ref):
    @pl.when(pl.program_id(1) == 0)
    def _():
        x = x_ref[...]
        xn_ref[...] = (x * _rms_scale(x) * g_ref[...]).astype(BF16)

    y = jnp.dot(xn_ref[...], w_ref[...], preferred_element_type=F32)
    o_ref[...] = jax.nn.gelu(y).astype(o_ref.dtype)


def _norm_matmul_rope_kernel(x_ref, g_ref, w_ref, tab_ref, o_ref, xn_ref, *, shift):
    @pl.when(pl.program_id(1) == 0)
    def _():
        x = x_ref[...]
        xn_ref[...] = (x * _rms_scale(x) * g_ref[...]).astype(BF16)

    y = jnp.dot(xn_ref[...], w_ref[...], preferred_element_type=F32)
    c, s1, s2 = tab_ref[0], tab_ref[1], tab_ref[2]
    for h in range(y.shape[1] // V7X_LANES):
        cols = slice(h * V7X_LANES, (h + 1) * V7X_LANES)
        yh = y[:, cols]
        r = (yh * c + pltpu.roll(yh, shift, 1) * s1
             + pltpu.roll(yh, V7X_LANES - shift, 1) * s2)
        o_ref[:, cols] = r.astype(o_ref.dtype)


def _norm_matmul(x, g, w, *, rope_tab=None, seq=None, rot_half=0):
    t, k = x.shape
    n = w.shape[1]
    tm = _tile(t if seq is None else seq, 1024)
    tn = _tile(n if rope_tab is None else n // 3, 1024)
    grid = (t // tm, n // tn)
    in_specs = [
        pl.BlockSpec((tm, k), lambda i, j: (i, 0)),
        pl.BlockSpec((1, k), lambda i, j: (0, 0)),
        pl.BlockSpec((k, tn), lambda i, j: (0, j)),
    ]
    args = [x, g.reshape(1, k), w]
    if rope_tab is None:
        body = _norm_matmul_gelu_kernel
    else:
        assert n % (3 * tn) == 0 and tn % V7X_LANES == 0
        section_tiles = n // (3 * tn)
        s_tiles = seq // tm
        in_specs.append(pl.BlockSpec(
            (None, 3, tm, V7X_LANES), lambda i, j: (j // section_tiles, 0, i % s_tiles, 0)))
        args.append(rope_tab)
        body = functools.partial(_norm_matmul_rope_kernel, shift=rot_half)
    return pl.pallas_call(
        body,
        grid=grid,
        in_specs=in_specs,
        out_specs=pl.BlockSpec((tm, tn), lambda i, j: (i, j)),
        out_shape=jax.ShapeDtypeStruct((t, n), BF16),
        scratch_shapes=[pltpu.VMEM((tm, k), BF16)],
        compiler_params=_params(("parallel", "arbitrary")),
        name="norm_matmul_gelu" if rope_tab is None else "norm_matmul_rope",
    )(*args)


def _matmul_residual_kernel(a_ref, w_ref, r_ref, o_ref):
    o_ref[...] = r_ref[...] + jnp.dot(a_ref[...], w_ref[...], preferred_element_type=F32)


def _matmul_residual(a, w, res):
    t, k = a.shape
    n = w.shape[1]
    tm, tn = _tile(t, 512), _tile(n, 2048)
    return pl.pallas_call(
        _matmul_residual_kernel,
        grid=(t // tm, n // tn),
        in_specs=[
            pl.BlockSpec((tm, k), lambda i, j: (i, 0)),
            pl.BlockSpec((k, tn), lambda i, j: (0, j)),
            pl.BlockSpec((tm, tn), lambda i, j: (i, j)),
        ],
        out_specs=pl.BlockSpec((tm, tn), lambda i, j: (i, j)),
        out_shape=jax.ShapeDtypeStruct((t, n), F32),
        compiler_params=_params(("parallel", "parallel")),
        name="matmul_residual",
    )(a, w, res)


def _gmlp_gate_kernel(u_ref, v_ref, lng_ref, lnb_ref, ws_ref, bst_ref, y_ref, vn_ref):
    n_groups, chunk, _ = ws_ref.shape
    tm, width = u_ref.shape
    gd = width // n_groups
    v = v_ref[...].astype(F32)
    vc = v - jnp.mean(v, axis=-1, keepdims=True)
    inv = lax.rsqrt(jnp.mean(vc * vc, axis=-1, keepdims=True) + EPS)
    vn_ref[...] = (vc * inv * lng_ref[...] + lnb_ref[...]).astype(BF16)
    causal = (lax.broadcasted_iota(I32, (chunk, chunk), 0) >= lax.broadcasted_iota(I32, (chunk, chunk), 1))
    for g in range(n_groups):
        w = jnp.where(causal, ws_ref[g], 0.0).astype(BF16)
        bias = bst_ref[:, g:g + 1]
        cols = slice(g * gd, (g + 1) * gd)
        for ci in range(tm // chunk):
            rows = slice(ci * chunk, (ci + 1) * chunk)
            sv = jnp.dot(w, vn_ref[rows, cols], preferred_element_type=F32) + bias
            y_ref[rows, cols] = (u_ref[rows, cols].astype(F32) * sv).astype(y_ref.dtype)


def _gmlp_gate(z, ln_g, ln_b, w_s, b_s):
    t, w2 = z.shape
    width = w2 // 2
    n_groups, chunk, _ = w_s.shape
    tm = _tile(t, 4 * chunk)
    return pl.pallas_call(
        _gmlp_gate_kernel,
        grid=(t // tm,),
        in_specs=[
            pl.BlockSpec((tm, width), lambda i: (i, 0)),
            pl.BlockSpec((tm, width), lambda i: (i, 1)),
            pl.BlockSpec((1, width), lambda i: (0, 0)),
            pl.BlockSpec((1, width), lambda i: (0, 0)),
            pl.BlockSpec((n_groups, chunk, chunk), lambda i: (0, 0, 0)),
            pl.BlockSpec((chunk, n_groups), lambda i: (0, 0)),
        ],
        out_specs=pl.BlockSpec((tm, width), lambda i: (i, 0)),
        out_shape=jax.ShapeDtypeStruct((t, width), BF16),
        scratch_shapes=[pltpu.VMEM((tm, width), BF16)],
        compiler_params=_params(("parallel",)),
        name="gmlp_gate",
    )(z, z, ln_g.reshape(1, width), ln_b.reshape(1, width), w_s, b_s.T)


def _diff_attn_kernel(lam_ref, q_ref, k_ref, v_ref, g_ref, o_ref, qs_ref, sw_ref, sn_ref, m_ref, acc_ref, *,
                      out_scale):
    qi = pl.program_id(2)
    tq, dh2 = q_ref.shape
    half = dh2 // 2
    n_wide = qi // 2
    has_odd = qi % 2 == 1

    q = q_ref[...]
    lane = lax.broadcasted_iota(I32, q.shape, 1)
    qs_ref[:tq, :] = jnp.where(lane < half, q, jnp.zeros_like(q))
    qs_ref[tq:, :] = jnp.where(lane >= half, q, jnp.zeros_like(q))
    m_ref[...] = jnp.full_like(m_ref, -jnp.inf)
    acc_ref[...] = jnp.zeros_like(acc_ref)

    def kv_rows(index, width):
        return pl.ds(pl.multiple_of(index * width, width), width)

    def scores(s_ref, slot, index):
        width = s_ref.shape[2]
        s_ref[slot] = lax.dot_general(qs_ref[...], k_ref[kv_rows(index, width), :], (((1,), (1,)), ((), ())),
                                      preferred_element_type=F32)

    def softmax_pv(s_ref, slot, index, masked=False):
        width = s_ref.shape[2]
        v = v_ref[kv_rows(index, width), :]
        v1 = jnp.concatenate([v, jnp.ones_like(v)], axis=1)
        for r in range(2):
            rs = slice(r * tq, (r + 1) * tq)
            s = s_ref[slot, rs, :]
            if masked:
                row = lax.broadcasted_iota(I32, s.shape, 0)
                col = lax.broadcasted_iota(I32, s.shape, 1)
                s = jnp.where(col <= row, s, -jnp.inf)
            m_old = m_ref[rs, :]
            m_new = jnp.maximum(m_old, jnp.max(s, axis=-1, keepdims=True))
            p = jnp.exp((s - m_new).astype(BF16))
            alpha = jnp.exp(m_old - m_new)
            acc_ref[rs, :] = alpha * acc_ref[rs, :] + jnp.dot(p, v1, preferred_element_type=F32)
            m_ref[rs, :] = m_new

    @pl.when(n_wide > 0)
    def _():
        scores(sw_ref, 0, 0)

    def wide_step(w, carry):
        slot = w % 2
        softmax_pv(sw_ref, slot, w)
        scores(sw_ref, 1 - slot, w + 1)
        return carry

    lax.fori_loop(0, n_wide - 1, wide_step, 0)
    last_wide = n_wide - 1

    @pl.when(jnp.logical_and(n_wide > 0, has_odd))
    def _():
        scores(sn_ref, 0, qi - 1)
        softmax_pv(sw_ref, last_wide % 2, last_wide)
        scores(sn_ref, 1, qi)
        softmax_pv(sn_ref, 0, qi - 1)
        softmax_pv(sn_ref, 1, qi, masked=True)

    @pl.when(jnp.logical_and(n_wide > 0, jnp.logical_not(has_odd)))
    def _():
        scores(sn_ref, 0, qi)
        softmax_pv(sw_ref, last_wide % 2, last_wide)
        softmax_pv(sn_ref, 0, qi, masked=True)

    @pl.when(jnp.logical_and(n_wide == 0, has_odd))
    def _():
        scores(sn_ref, 0, qi - 1)
        scores(sn_ref, 1, qi)
        softmax_pv(sn_ref, 0, qi - 1)
        softmax_pv(sn_ref, 1, qi, masked=True)

    @pl.when(jnp.logical_and(n_wide == 0, jnp.logical_not(has_odd)))
    def _():
        scores(sn_ref, 0, qi)
        softmax_pv(sn_ref, 0, qi, masked=True)

    o = acc_ref[:, :dh2] / acc_ref[:, dh2:]
    o = o[:tq] - lam_ref[0] * o[tq:]
    o = o * _rms_scale(o) * g_ref[...] * out_scale
    o_ref[...] = o.astype(o_ref.dtype)


def _diff_attention(qkv, lam, subln_g, *, batch, seq, n_heads, out_scale):
    t = qkv.shape[0]
    dh2 = qkv.shape[1] // (3 * n_heads)
    tq = _tile(seq, ATTN_Q_TILE)
    nq = seq // tq
    return pl.pallas_call(
        functools.partial(_diff_attn_kernel, out_scale=out_scale),
        grid=(batch, n_heads, nq),
        in_specs=[
            pl.BlockSpec(memory_space=pltpu.SMEM),
            pl.BlockSpec((tq, dh2), lambda b, h, i: (b * nq + i, h)),
            pl.BlockSpec((seq, dh2), lambda b, h, i: (b, n_heads + h)),
            pl.BlockSpec((seq, dh2), lambda b, h, i: (b, 2 * n_heads + h)),
            pl.BlockSpec((1, dh2), lambda b, h, i: (0, 0)),
        ],
        out_specs=pl.BlockSpec((tq, dh2), lambda b, h, i: (b * nq + i, h)),
        out_shape=jax.ShapeDtypeStruct((t, n_heads * dh2), BF16),
        scratch_shapes=[
            pltpu.VMEM((2 * tq, dh2), BF16),
            pltpu.VMEM((2, 2 * tq, 2 * tq), F32),
            pltpu.VMEM((2, 2 * tq, tq), F32),
            pltpu.VMEM((2 * tq, 1), F32),
            pltpu.VMEM((2 * tq, 2 * dh2), F32),
        ],
        compiler_params=_params(("parallel", "parallel", "arbitrary")),
        name="diff_attention",
    )(lam.reshape(1), qkv, qkv, qkv, subln_g.reshape(1, dh2))


def _router_kernel(x_ref, g_ref, rw_ref, rb_ref, hp_ref, r_ref, cnt_ref, carry_ref):
    @pl.when(pl.program_id(0) == 0)
    def _():
        carry_ref[...] = jnp.zeros_like(carry_ref)

    x = x_ref[...]
    tm, d = x.shape
    n_exp = rw_ref.shape[1]
    h = x * _rms_scale(x) * g_ref[...]
    h_hi = h.astype(BF16)
    h_hi32 = h_hi.astype(F32)
    half = d // 2
    lo_word = pltpu.bitcast(h_hi32[:, :half], U32) >> 16
    hi_word = pltpu.bitcast(h_hi32[:, half:], U32) & jnp.uint32(HI16)
    hp_ref[...] = lo_word | hi_word

    h_lo = (h - h_hi32).astype(BF16)
    rw = rw_ref[...]
    w_hi = rw.astype(BF16)
    w_lo = (rw - w_hi.astype(F32)).astype(BF16)
    logits = (jnp.dot(h_hi, w_hi, preferred_element_type=F32)
              + jnp.dot(h_lo, w_hi, preferred_element_type=F32)
              + jnp.dot(h_hi, w_lo, preferred_element_type=F32)) + rb_ref[...]

    e_iota = lax.broadcasted_iota(I32, (tm, n_exp), 1).astype(F32)
    work = logits
    vals, onehots, idxs = [], [], []
    for _ in range(TOP_K):
        m = jnp.max(work, axis=-1, keepdims=True)
        idx = jnp.min(jnp.where(work == m, e_iota, float(n_exp)), axis=-1, keepdims=True)
        hit = e_iota == idx
        vals.append(m)
        idxs.append(idx)
        onehots.append(hit)
        work = jnp.where(hit, -jnp.inf, work)
    exps = [jnp.exp(v - vals[0]) for v in vals]
    denom = functools.reduce(lambda a, b: a + b, exps)

    chosen = functools.reduce(jnp.logical_or, onehots)
    chosen_f = jnp.where(chosen, 1.0, 0.0)
    earlier = (lax.broadcasted_iota(I32, (tm, tm), 0) > lax.broadcasted_iota(I32, (tm, tm), 1))
    before = jnp.dot(jnp.where(earlier, 1.0, 0.0).astype(BF16), chosen_f.astype(BF16),
                     preferred_element_type=F32) + carry_ref[...]
    carry_ref[...] = carry_ref[...] + jnp.sum(chosen_f, axis=0, keepdims=True)
    cnt_ref[...] = carry_ref[...]

    lane = lax.broadcasted_iota(I32, r_ref.shape, 1)
    out = jnp.zeros(r_ref.shape, F32)
    for k in range(TOP_K):
        rank = jnp.sum(jnp.where(onehots[k], before, 0.0), axis=-1, keepdims=True)
        out = jnp.where(lane == k, idxs[k], out)
        out = jnp.where(lane == TOP_K + k, rank, out)
        out = jnp.where(lane == 2 * TOP_K + k, exps[k] / denom, out)
    r_ref[...] = out


def _router(x, g, router_w, router_b):
    t, d = x.shape
    n_exp = router_w.shape[1]
    tm = _tile(t, 512)
    return pl.pallas_call(
        _router_kernel,
        grid=(t // tm,),
        in_specs=[
            pl.BlockSpec((tm, d), lambda i: (i, 0)),
            pl.BlockSpec((1, d), lambda i: (0, 0)),
            pl.BlockSpec((d, n_exp), lambda i: (0, 0)),
            pl.BlockSpec((1, n_exp), lambda i: (0, 0)),
        ],
        out_specs=[
            pl.BlockSpec((tm, d // 2), lambda i: (i, 0)),
            pl.BlockSpec((tm, V7X_LANES), lambda i: (i, 0)),
            pl.BlockSpec((1, n_exp), lambda i: (0, 0)),
        ],
        out_shape=[
            jax.ShapeDtypeStruct((t, d // 2), U32),
            jax.ShapeDtypeStruct((t, V7X_LANES), F32),
            jax.ShapeDtypeStruct((1, n_exp), F32),
        ],
        scratch_shapes=[pltpu.VMEM((1, n_exp), F32)],
        compiler_params=_params(("arbitrary",)),
        name="moe_router",
    )(x, g.reshape(1, d), router_w, router_b.reshape(1, n_exp))


def _dispatch_kernel(dest_ref, hp_ref, xs_init_ref, xs_ref, sem):
    del xs_init_ref
    tm = hp_ref.shape[0]

    def row_copy(r, k):
        return pltpu.make_async_copy(hp_ref.at[pl.ds(r, 1), :],
                                     xs_ref.at[pl.ds(dest_ref[r * TOP_K + k], 1), :], sem)

    def issue(r, carry):
        for k in range(TOP_K):
            row_copy(r, k).start(priority=k % 2)
        return carry

    def drain(r, carry):
        for k in range(TOP_K):
            row_copy(r, k).wait()
        return carry

    lax.fori_loop(0, tm, issue, 0)
    lax.fori_loop(0, tm, drain, 0)


def _dispatch(hp, dest_flat, n_slots):
    t, dw = hp.shape
    tm = _tile(t, 256)
    xs_init = jnp.zeros((n_slots, dw), U32)
    return pl.pallas_call(
        _dispatch_kernel,
        grid=(t // tm,),
        in_specs=[
            pl.BlockSpec((tm * TOP_K,), lambda i: (i,), memory_space=pltpu.SMEM),
            pl.BlockSpec((tm, dw), lambda i: (i, 0)),
            pl.BlockSpec(memory_space=pl.ANY),
        ],
        out_specs=pl.BlockSpec(memory_space=pl.ANY),
        out_shape=jax.ShapeDtypeStruct((n_slots, dw), U32),
        scratch_shapes=[pltpu.SemaphoreType.DMA(())],
        input_output_aliases={2: 0},
        compiler_params=_params(("arbitrary",)),
        name="moe_dispatch",
    )(dest_flat, hp, xs_init)


def _experts_kernel(te_ref, ns_ref, nu_ref, xs_ref, wgu_ref, wd_ref, bgu_ref, bd_ref, y_ref, x_ref, *, sub_rows):
    del te_ref, nu_ref
    i, c = pl.program_id(0), pl.program_id(1)
    n_sub = ns_ref[i]
    bm = y_ref.shape[0]
    half_words = xs_ref.shape[1]
    cw = wgu_ref.shape[1]
    hw = cw // 2

    @pl.when(c == 0)
    def _():
        words = xs_ref[...]
        x_ref[:, :half_words] = pltpu.bitcast(words << 16, F32).astype(BF16)
        x_ref[:, half_words:] = pltpu.bitcast(words & jnp.uint32(HI16), F32).astype(BF16)
        for sb in range(bm // sub_rows):
            y_ref[sb * sub_rows:(sb + 1) * sub_rows, :] = jnp.broadcast_to(
                jnp.where(sb < n_sub, bd_ref[...], 0.0), (sub_rows, y_ref.shape[1]))

    def token_rows(m_rows):
        wgu = wgu_ref[...].astype(BF16)
        lo = pltpu.bitcast(wd_ref[:hw // 2, :].astype(BF16).astype(F32), U32) >> 16
        hi = pltpu.bitcast(wd_ref[hw // 2:, :].astype(BF16).astype(F32), U32) & jnp.uint32(HI16)
        wdi = pltpu.bitcast(lo | hi, BF16)
        rows = slice(0, m_rows)
        even = (lax.broadcasted_iota(I32, (m_rows, V7X_LANES), 1) & 1) == 0
        gu = jnp.dot(x_ref[rows, :], wgu, preferred_element_type=F32) + bgu_ref[...]
        gs = jnp.minimum(gu, SWIGLU_LIMIT)
        gate = gs * jax.nn.sigmoid(SWIGLU_ALPHA * gs)
        up = jnp.clip(gu, -SWIGLU_LIMIT, SWIGLU_LIMIT) + 1.0
        prod = []
        for v in range(cw // V7X_LANES):
            cols = slice(v * V7X_LANES, (v + 1) * V7X_LANES)
            prod.append(gate[:, cols] * pltpu.roll(up[:, cols], V7X_LANES - 1, 1))
        nv = hw // V7X_LANES
        act = jnp.concatenate(
            [jnp.where(even, prod[v], pltpu.roll(prod[nv + v], 1, 1)) for v in range(nv)], axis=1)
        y_ref[rows, :] += jnp.dot(act.astype(BF16), wdi, preferred_element_type=F32)

    for k in range(1, bm // sub_rows + 1):
        @pl.when(n_sub == k)
        def _(k=k):
            token_rows(k * sub_rows)


def _experts(xs, tile_expert, tile_subs, n_used, w_gu, w_d, b_gu, b_d, *, layer, bm, sub_rows):
    n_slots, dw = xs.shape
    depth, n_exp, d, f2 = w_gu.shape
    cw = _tile(f2, MOE_GU_CHUNK)
    assert cw % (2 * V7X_LANES) == 0
    nt, nc = n_slots // bm, f2 // cw

    def tile_idx(i, nu):
        return jnp.minimum(i, nu[0] - 1)

    def chunk_idx(i, c, nu):
        return jnp.where(i < nu[0], c, nc - 1)

    def expert(i, te, nu):
        return te[tile_idx(i, nu)]

    grid_spec = pltpu.PrefetchScalarGridSpec(
        num_scalar_prefetch=3,
        grid=(nt, nc),
        in_specs=[
            pl.BlockSpec((bm, dw), lambda i, c, te, ns, nu: (tile_idx(i, nu), 0)),
            pl.BlockSpec((None, None, d, cw),
                         lambda i, c, te, ns, nu: (layer, expert(i, te, nu), 0, chunk_idx(i, c, nu))),
            pl.BlockSpec((None, None, cw // 2, d),
                         lambda i, c, te, ns, nu: (layer, expert(i, te, nu), chunk_idx(i, c, nu), 0)),
            pl.BlockSpec((None, None, 1, cw),
                         lambda i, c, te, ns, nu: (layer, expert(i, te, nu), 0, chunk_idx(i, c, nu))),
            pl.BlockSpec((None, None, 1, d), lambda i, c, te, ns, nu: (layer, expert(i, te, nu), 0, 0)),
        ],
        out_specs=pl.BlockSpec((bm, d), lambda i, c, te, ns, nu: (i, 0)),
        scratch_shapes=[pltpu.VMEM((bm, d), BF16)],
    )
    return pl.pallas_call(
        functools.partial(_experts_kernel, sub_rows=sub_rows),
        grid_spec=grid_spec,
        out_shape=jax.ShapeDtypeStruct((n_slots, d), F32),
        compiler_params=_params(("arbitrary", "arbitrary"), VMEM_LIMIT_EXPERTS_BYTES),
        name="moe_experts",
    )(tile_expert, tile_subs, n_used, xs, w_gu, w_d,
      b_gu.reshape(depth, n_exp, 1, f2), b_d.reshape(depth, n_exp, 1, d))


def _combine_kernel(dest_ref, x_ref, r_ref, y_ref, *rest, final_norm):
    if final_norm:
        g_ref, o_ref, buf_ref, sem = rest
    else:
        o_ref, buf_ref, sem = rest
    tm = x_ref.shape[0]
    n_parts = sem.shape[0]
    part = tm // n_parts

    def row_copy(r, k, p):
        return pltpu.make_async_copy(y_ref.at[pl.ds(dest_ref[r * TOP_K + k], 1), :],
                                     buf_ref.at[k, pl.ds(r, 1), :], sem.at[p])

    for p in range(n_parts):
        def issue(r, carry, p=p):
            for k in range(TOP_K):
                row_copy(r, k, p).start(priority=k % 2)
            return carry
        lax.fori_loop(p * part, (p + 1) * part, issue, 0)

    for p in range(n_parts):
        def drain(r, carry, p=p):
            for k in range(TOP_K):
                row_copy(r, k, p).wait()
            return carry
        lax.fori_loop(p * part, (p + 1) * part, drain, 0)

        rows = slice(p * part, (p + 1) * part)
        acc = x_ref[rows, :]
        for k in range(TOP_K):
            acc = acc + r_ref[rows, 2 * TOP_K + k:2 * TOP_K + k + 1] * buf_ref[k, rows, :]
        if final_norm:
            acc = acc * _rms_scale(acc) * g_ref[...]
        o_ref[rows, :] = acc


def _combine(x, r, y, dest_flat, final_g=None):
    t, d = x.shape
    tm = _tile(t, 256)
    in_specs = [
        pl.BlockSpec((tm * TOP_K,), lambda i: (i,), memory_space=pltpu.SMEM),
        pl.BlockSpec((tm, d), lambda i: (i, 0)),
        pl.BlockSpec((tm, V7X_LANES), lambda i: (i, 0)),
        pl.BlockSpec(memory_space=pl.ANY),
    ]
    args = [dest_flat, x, r, y]
    if final_g is not None:
        in_specs.append(pl.BlockSpec((1, d), lambda i: (0, 0)))
        args.append(final_g.reshape(1, d))
    return pl.pallas_call(
        functools.partial(_combine_kernel, final_norm=final_g is not None),
        grid=(t // tm,),
        in_specs=in_specs,
        out_specs=pl.BlockSpec((tm, d), lambda i: (i, 0)),
        out_shape=jax.ShapeDtypeStruct((t, d), F32),
        scratch_shapes=[pltpu.VMEM((TOP_K, tm, d), F32), pltpu.SemaphoreType.DMA((COMBINE_PARTS,))],
        compiler_params=_params(("arbitrary",)),
        name="moe_combine",
    )(*args)


def _moe(x, norm_g, router_w, router_b, w_gate_up, b_gate_up, w_down, b_down, *, layer, final_g=None):
    t, d = x.shape
    n_exp = router_w.shape[1]
    bm = MOE_ROW_TILE
    sub_rows = _tile(bm, MOE_SUB_ROWS)
    n_tiles = t * TOP_K // bm + n_exp
    hp, r, cnt = _router(x, norm_g, router_w, router_b)

    top_i = r[:, :TOP_K].astype(I32)
    rank = r[:, TOP_K:2 * TOP_K].astype(I32)
    counts = cnt[0].astype(I32)
    exp_tiles = (counts + bm - 1) // bm
    tile_end = jnp.cumsum(exp_tiles)
    tile_start = tile_end - exp_tiles
    dest_flat = (tile_start[top_i] * bm + rank).reshape(-1)
    n_used = tile_end[-1:].astype(I32)
    tile_ids = jnp.arange(n_tiles, dtype=I32)
    tile_expert = jnp.minimum(jnp.sum(tile_end[None, :] <= tile_ids[:, None], axis=1), n_exp - 1).astype(I32)
    tile_rows = jnp.clip(counts[tile_expert] - (tile_ids - tile_start[tile_expert]) * bm, 0, bm)
    tile_subs = jnp.where(tile_ids < n_used[0], (tile_rows + sub_rows - 1) // sub_rows, 0).astype(I32)

    xs = _dispatch(hp, dest_flat, n_tiles * bm)
    y = _experts(xs, tile_expert, tile_subs, n_used, w_gate_up, w_down, b_gate_up, b_down,
                 layer=layer, bm=bm, sub_rows=sub_rows)
    return _combine(x, r, y, dest_flat, final_g)


def _rope_tables(seq, dh, rot_dim, q_scale):
    half = rot_dim // 2
    inv_freq = ROPE_THETA ** (-jnp.arange(0, rot_dim, 2, dtype=F32) / rot_dim)
    ang = jnp.arange(seq, dtype=F32)[:, None] * inv_freq[None, :]
    cos, sin = jnp.cos(ang), jnp.sin(ang)
    ones = jnp.ones((seq, dh - rot_dim), F32)
    zeros_rest = jnp.zeros((seq, dh - rot_dim), F32)
    zeros_half = jnp.zeros((seq, half), F32)
    c = jnp.concatenate([cos, cos, ones], axis=1)
    s1 = jnp.concatenate([zeros_half, sin, zeros_rest], axis=1)
    s2 = jnp.concatenate([-sin, zeros_half, zeros_rest], axis=1)
    k_tab = jnp.stack([jnp.tile(a, (1, 2)) for a in (c, s1, s2)])
    v_tab = jnp.stack([jnp.ones_like(k_tab[0]), jnp.zeros_like(k_tab[0]), jnp.zeros_like(k_tab[0])])
    return jnp.stack([k_tab * q_scale, k_tab, v_tab])


def _gmlp_layer(x, norm_g, w_in, ln_g, ln_b, w_s, b_s, w_out):
    z = _norm_matmul(x, norm_g, w_in.astype(BF16))
    y = _gmlp_gate(z, ln_g, ln_b, w_s, b_s)
    return _matmul_residual(y, w_out.astype(BF16), x)


def _attn_layer(x, norm_g, w_qkv, lq1, lk1, lq2, lk2, subln_g, w_o, lambda_init, *, batch, seq):
    d = x.shape[1]
    dh = lq1.shape[0]
    n_heads = d // (2 * dh)
    rot_dim = dh // 4
    tab = _rope_tables(seq, dh, rot_dim, dh ** -0.5)
    qkv = _norm_matmul(x, norm_g, w_qkv.astype(BF16), rope_tab=tab, seq=seq, rot_half=rot_dim // 2)
    lam = jnp.exp(jnp.sum(lq1 * lk1)) - jnp.exp(jnp.sum(lq2 * lk2)) + lambda_init
    o = _diff_attention(qkv, lam.astype(F32), subln_g, batch=batch, seq=seq, n_heads=n_heads,
                        out_scale=1.0 - lambda_init)
    return _matmul_residual(o, w_o.astype(BF16), x)


def kernel(x, attn_norm_g, ffn_norm_g, final_norm_g, a_w_in, a_ln_g, a_ln_b, a_w_s, a_b_s, a_w_out,
           b_w_qkv, b_lambda_q1, b_lambda_k1, b_lambda_q2, b_lambda_k2, b_subln_g, b_w_o,
           router_w, router_b, w_gate_up, b_gate_up, w_down, b_down):
    batch, seq, d = x.shape
    depth = attn_norm_g.shape[0]
    n_mixers = 2
    h = x.reshape(batch * seq, d)
    for i in range(depth):
        j = i // n_mixers
        if i % n_mixers == 0:
            h = _gmlp_layer(h, attn_norm_g[i], a_w_in[j], a_ln_g[j], a_ln_b[j], a_w_s[j], a_b_s[j], a_w_out[j])
        else:
            lambda_init = 0.8 - 0.6 * math.exp(-0.3 * i)
            h = _attn_layer(h, attn_norm_g[i], b_w_qkv[j], b_lambda_q1[j], b_lambda_k1[j], b_lambda_q2[j],
                            b_lambda_k2[j], b_subln_g[j], b_w_o[j], lambda_init, batch=batch, seq=seq)
        h = _moe(h, ffn_norm_g[i], router_w[i], router_b[i], w_gate_up, b_gate_up, w_down, b_down,
                 layer=i, final_g=final_norm_g if i == depth - 1 else None)
    return h.reshape(batch, seq, d)
```

```python
import functools
import math

import jax
import jax.numpy as jnp
from jax import lax
from jax.experimental import pallas as pl
from jax.experimental.pallas import tpu as pltpu

F32 = jnp.float32
BF16 = jnp.bfloat16
I32 = jnp.int32
U32 = jnp.uint32

EPS = 1e-5
TOP_K = 4
ROPE_THETA = 500000.0
SWIGLU_LIMIT = 7.0
SWIGLU_ALPHA = 1.702

V7X_LANES = 128
V7X_VMEM_BYTES = 64 * 1024 * 1024
VMEM_LIMIT_BYTES = V7X_VMEM_BYTES * 7 // 8
VMEM_LIMIT_EXPERTS_BYTES = V7X_VMEM_BYTES * 15 // 16
HI16 = 0xFFFF0000

ATTN_Q_TILE = 512
COMBINE_PARTS = 1
MOE_ROW_TILE = 1024
MOE_SUB_ROWS = 128
MOE_GU_CHUNK = 1024


def _tile(dim, want):
    t = min(dim, want)
    while dim % t:
        t //= 2
    return t


def _params(semantics, vmem_limit_bytes=VMEM_LIMIT_BYTES):
    return pltpu.CompilerParams(dimension_semantics=semantics, vmem_limit_bytes=vmem_limit_bytes)


def _rms_scale(x):
    return lax.rsqrt(jnp.mean(x * x, axis=-1, keepdims=True) + EPS)


def _norm_matmul_gelu_kernel(x_ref, g_ref, w_ref, o_ref, xn_ref):
    @pl.when(pl.program_id(1) == 0)
    def _():
        x = x_ref[...]
        xn_ref[...] = (x * _rms_scale(x) * g_ref[...]).astype(BF16)

    y = jnp.dot(xn_ref[...], w_ref[...], preferred_element_type=F32)
    o_ref[...] = jax.nn.gelu(y).astype(o_ref.dtype)


def _norm_matmul_rope_kernel(x_ref, g_ref, w_ref, tab_ref, o_ref, xn_ref, *, shift):
    @pl.when(pl.program_id(1) == 0)
    def _():
        x = x_ref[...]
        xn_ref[...] = (x * _rms_scale(x) * g_ref[...]).astype(BF16)

    y = jnp.dot(xn_ref[...], w_ref[...], preferred_element_type=F32)
    c, s1, s2 = tab_ref[0], tab_ref[1], tab_ref[2]
    for h in range(y.shape[1] // V7X_LANES):
        cols = slice(h * V7X_LANES, (h + 1) * V7X_LANES)
        yh = y[:, cols]
        r = (yh * c + pltpu.roll(yh, shift, 1) * s1
             + pltpu.roll(yh, V7X_LANES - shift, 1) * s2)
        o_ref[:, cols] = r.astype(o_ref.dtype)


def _norm_matmul(x, g, w, *, rope_tab=None, seq=None, rot_half=0):
    t, k = x.shape
    n = w.shape[1]
    tm = _tile(t if seq is None else seq, 1024)
    tn = _tile(n if rope_tab is None else n // 3, 1024)
    grid = (t // tm, n // tn)
    in_specs = [
        pl.BlockSpec((tm, k), lambda i, j: (i, 0)),
        pl.BlockSpec((1, k), lambda i, j: (0, 0)),
        pl.BlockSpec((k, tn), lambda i, j: (0, j)),
    ]
    args = [x, g.reshape(1, k), w]
    if rope_tab is None:
        body = _norm_matmul_gelu_kernel
    else:
        assert n % (3 * tn) == 0 and tn % V7X_LANES == 0
        section_tiles = n // (3 * tn)
        s_tiles = seq // tm
        in_specs.append(pl.BlockSpec(
            (None, 3, tm, V7X_LANES), lambda i, j: (j // section_tiles, 0, i % s_tiles, 0)))
        args.append(rope_tab)
        body = functools.partial(_norm_matmul_rope_kernel, shift=rot_half)
    return pl.pallas_call(
        body,
        grid=grid,
        in_specs=in_specs,
        out_specs=pl.BlockSpec((tm, tn), lambda i, j: (i, j)),
        out_shape=jax.ShapeDtypeStruct((t, n), BF16),
        scratch_shapes=[pltpu.VMEM((tm, k), BF16)],
        compiler_params=_params(("parallel", "arbitrary")),
        name="norm_matmul_gelu" if rope_tab is None else "norm_matmul_rope",
    )(*args)


def _matmul_residual_kernel(a_ref, w_ref, r_ref, o_ref):
    o_ref[...] = r_ref[...] + jnp.dot(a_ref[...], w_ref[...], preferred_element_type=F32)


def _matmul_residual(a, w, res):
    t, k = a.shape
    n = w.shape[1]
    tm, tn = _tile(t, 512), _tile(n, 2048)
    return pl.pallas_call(
        _matmul_residual_kernel,
        grid=(t // tm, n // tn),
        in_specs=[
            pl.BlockSpec((tm, k), lambda i, j: (i, 0)),
            pl.BlockSpec((k, tn), lambda i, j: (0, j)),
            pl.BlockSpec((tm, tn), lambda i, j: (i, j)),
        ],
        out_specs=pl.BlockSpec((tm, tn), lambda i, j: (i, j)),
        out_shape=jax.ShapeDtypeStruct((t, n), F32),
        compiler_params=_params(("parallel", "parallel")),
        name="matmul_residual",
    )(a, w, res)


def _gmlp_gate_kernel(u_ref, v_ref, lng_ref, lnb_ref, ws_ref, bst_ref, y_ref, vn_ref):
    n_groups, chunk, _ = ws_ref.shape
    tm, width = u_ref.shape
    gd = width // n_groups
    v = v_ref[...].astype(F32)
    vc = v - jnp.mean(v, axis=-1, keepdims=True)
    inv = lax.rsqrt(jnp.mean(vc * vc, axis=-1, keepdims=True) + EPS)
    vn_ref[...] = (vc * inv * lng_ref[...] + lnb_ref[...]).astype(BF16)
    causal = (lax.broadcasted_iota(I32, (chunk, chunk), 0) >= lax.broadcasted_iota(I32, (chunk, chunk), 1))
    for g in range(n_groups):
        w = jnp.where(causal, ws_ref[g], 0.0).astype(BF16)
        bias = bst_ref[:, g:g + 1]
        cols = slice(g * gd, (g + 1) * gd)
        for ci in range(tm // chunk):
            rows = slice(ci * chunk, (ci + 1) * chunk)
            sv = jnp.dot(w, vn_ref[rows, cols], preferred_element_type=F32) + bias
            y_ref[rows, cols] = (u_ref[rows, cols].astype(F32) * sv).astype(y_ref.dtype)


def _gmlp_gate(z, ln_g, ln_b, w_s, b_s):
    t, w2 = z.shape
    width = w2 // 2
    n_groups, chunk, _ = w_s.shape
    tm = _tile(t, 4 * chunk)
    return pl.pallas_call(
        _gmlp_gate_kernel,
        grid=(t // tm,),
        in_specs=[
            pl.BlockSpec((tm, width), lambda i: (i, 0)),
            pl.BlockSpec((tm, width), lambda i: (i, 1)),
            pl.BlockSpec((1, width), lambda i: (0, 0)),
            pl.BlockSpec((1, width), lambda i: (0, 0)),
            pl.BlockSpec((n_groups, chunk, chunk), lambda i: (0, 0, 0)),
            pl.BlockSpec((chunk, n_groups), lambda i: (0, 0)),
        ],
        out_specs=pl.BlockSpec((tm, width), lambda i: (i, 0)),
        out_shape=jax.ShapeDtypeStruct((t, width), BF16),
        scratch_shapes=[pltpu.VMEM((tm, width), BF16)],
        compiler_params=_params(("parallel",)),
        name="gmlp_gate",
    )(z, z, ln_g.reshape(1, width), ln_b.reshape(1, width), w_s, b_s.T)


def _diff_attn_kernel(lam_ref, q_ref, k_ref, v_ref, g_ref, o_ref, qs_ref, sw_ref, sn_ref, m_ref, acc_ref, *,
                      out_scale):
    qi = pl.program_id(2)
    tq, dh2 = q_ref.shape
    half = dh2 // 2
    n_wide = qi // 2
    has_odd = qi % 2 == 1

    q = q_ref[...]
    lane = lax.broadcasted_iota(I32, q.shape, 1)
    qs_ref[:tq, :] = jnp.where(lane < half, q, jnp.zeros_like(q))
    qs_ref[tq:, :] = jnp.where(lane >= half, q, jnp.zeros_like(q))
    m_ref[...] = jnp.full_like(m_ref, -jnp.inf)
    acc_ref[...] = jnp.zeros_like(acc_ref)

    def kv_rows(index, width):
        return pl.ds(pl.multiple_of(index * width, width), width)

    def scores(s_ref, slot, index):
        width = s_ref.shape[2]
        s_ref[slot] = lax.dot_general(qs_ref[...], k_ref[kv_rows(index, width), :], (((1,), (1,)), ((), ())),
                                      preferred_element_type=F32)

    def softmax_pv(s_ref, slot, index, masked=False):
        width = s_ref.shape[2]
        v = v_ref[kv_rows(index, width), :]
        v1 = jnp.concatenate([v, jnp.ones_like(v)], axis=1)
        for r in range(2):
            rs = slice(r * tq, (r + 1) * tq)
            s = s_ref[slot, rs, :]
            if masked:
                row = lax.broadcasted_iota(I32, s.shape, 0)
                col = lax.broadcasted_iota(I32, s.shape, 1)
                s = jnp.where(col <= row, s, -jnp.inf)
            m_old = m_ref[rs, :]
            m_new = jnp.maximum(m_old, jnp.max(s, axis=-1, keepdims=True))
            p = jnp.exp((s - m_new).astype(BF16))
            alpha = jnp.exp(m_old - m_new)
            acc_ref[rs, :] = alpha * acc_ref[rs, :] + jnp.dot(p, v1, preferred_element_type=F32)
            m_ref[rs, :] = m_new

    @pl.when(n_wide > 0)
    def _():
        scores(sw_ref, 0, 0)

    def wide_step(w, carry):
        slot = w % 2
        softmax_pv(sw_ref, slot, w)
        scores(sw_ref, 1 - slot, w + 1)
        return carry

    lax.fori_loop(0, n_wide - 1, wide_step, 0)
    last_wide = n_wide - 1

    @pl.when(jnp.logical_and(n_wide > 0, has_odd))
    def _():
        scores(sn_ref, 0, qi - 1)
        softmax_pv(sw_ref, last_wide % 2, last_wide)
        scores(sn_ref, 1, qi)
        softmax_pv(sn_ref, 0, qi - 1)
        softmax_pv(sn_ref, 1, qi, masked=True)

    @pl.when(jnp.logical_and(n_wide > 0, jnp.logical_not(has_odd)))
    def _():
        scores(sn_ref, 0, qi)
        softmax_pv(sw_ref, last_wide % 2, last_wide)
        softmax_pv(sn_ref, 0, qi, masked=True)

    @pl.when(jnp.logical_and(n_wide == 0, has_odd))
    def _():
        scores(sn_ref, 0, qi - 1)
        scores(sn_ref, 1, qi)
        softmax_pv(sn_ref, 0, qi - 1)
        softmax_pv(sn_ref, 1, qi, masked=True)

    @pl.when(jnp.logical_and(n_wide == 0, jnp.logical_not(has_odd)))
    def _():
        scores(sn_ref, 0, qi)
        softmax_pv(sn_ref, 0, qi, masked=True)

    o = acc_ref[:, :dh2] / acc_ref[:, dh2:]
    o = o[:tq] - lam_ref[0] * o[tq:]
    o = o * _rms_scale(o) * g_ref[...] * out_scale
    o_ref[...] = o.astype(o_ref.dtype)


def _diff_attention(qkv, lam, subln_g, *, batch, seq, n_heads, out_scale):
    t = qkv.shape[0]
    dh2 = qkv.shape[1] // (3 * n_heads)
    tq = _tile(seq, ATTN_Q_TILE)
    nq = seq // tq
    return pl.pallas_call(
        functools.partial(_diff_attn_kernel, out_scale=out_scale),
        grid=(batch, n_heads, nq),
        in_specs=[
            pl.BlockSpec(memory_space=pltpu.SMEM),
            pl.BlockSpec((tq, dh2), lambda b, h, i: (b * nq + i, h)),
            pl.BlockSpec((seq, dh2), lambda b, h, i: (b, n_heads + h)),
            pl.BlockSpec((seq, dh2), lambda b, h, i: (b, 2 * n_heads + h)),
            pl.BlockSpec((1, dh2), lambda b, h, i: (0, 0)),
        ],
        out_specs=pl.BlockSpec((tq, dh2), lambda b, h, i: (b * nq + i, h)),
        out_shape=jax.ShapeDtypeStruct((t, n_heads * dh2), BF16),
        scratch_shapes=[
            pltpu.VMEM((2 * tq, dh2), BF16),
            pltpu.VMEM((2, 2 * tq, 2 * tq), F32),
            pltpu.VMEM((2, 2 * tq, tq), F32),
            pltpu.VMEM((2 * tq, 1), F32),
            pltpu.VMEM((2 * tq, 2 * dh2), F32),
        ],
        compiler_params=_params(("parallel", "parallel", "arbitrary")),
        name="diff_attention",
    )(lam.reshape(1), qkv, qkv, qkv, subln_g.reshape(1, dh2))


def _router_kernel(x_ref, g_ref, rw_ref, rb_ref, hp_ref, r_ref, cnt_ref, carry_ref):
    @pl.when(pl.program_id(0) == 0)
    def _():
        carry_ref[...] = jnp.zeros_like(carry_ref)

    x = x_ref[...]
    tm, d = x.shape
    n_exp = rw_ref.shape[1]
    h = x * _rms_scale(x) * g_ref[...]
    h_hi = h.astype(BF16)
    h_hi32 = h_hi.astype(F32)
    half = d // 2
    lo_word = pltpu.bitcast(h_hi32[:, :half], U32) >> 16
    hi_word = pltpu.bitcast(h_hi32[:, half:], U32) & jnp.uint32(HI16)
    hp_ref[...] = lo_word | hi_word

    h_lo = (h - h_hi32).astype(BF16)
    rw = rw_ref[...]
    w_hi = rw.astype(BF16)
    w_lo = (rw - w_hi.astype(F32)).astype(BF16)
    logits = (jnp.dot(h_hi, w_hi, preferred_element_type=F32)
              + jnp.dot(h_lo, w_hi, preferred_element_type=F32)
              + jnp.dot(h_hi, w_lo, preferred_element_type=F32)) + rb_ref[...]

    e_iota = lax.broadcasted_iota(I32, (tm, n_exp), 1).astype(F32)
    work = logits
    vals, onehots, idxs = [], [], []
    for _ in range(TOP_K):
        m = jnp.max(work, axis=-1, keepdims=True)
        idx = jnp.min(jnp.where(work == m, e_iota, float(n_exp)), axis=-1, keepdims=True)
        hit = e_iota == idx
        vals.append(m)
        idxs.append(idx)
        onehots.append(hit)
        work = jnp.where(hit, -jnp.inf, work)
    exps = [jnp.exp(v - vals[0]) for v in vals]
    denom = functools.reduce(lambda a, b: a + b, exps)

    chosen = functools.reduce(jnp.logical_or, onehots)
    chosen_f = jnp.where(chosen, 1.0, 0.0)
    earlier = (lax.broadcasted_iota(I32, (tm, tm), 0) > lax.broadcasted_iota(I32, (tm, tm), 1))
    before = jnp.dot(jnp.where(earlier, 1.0, 0.0).astype(BF16), chosen_f.astype(BF16),
                     preferred_element_type=F32) + carry_ref[...]
    carry_ref[...] = carry_ref[...] + jnp.sum(chosen_f, axis=0, keepdims=True)
    cnt_ref[...] = carry_ref[...]

    lane = lax.broadcasted_iota(I32, r_ref.shape, 1)
    out = jnp.zeros(r_ref.shape, F32)
    for k in range(TOP_K):
        rank = jnp.sum(jnp.where(onehots[k], before, 0.0), axis=-1, keepdims=True)
        out = jnp.where(lane == k, idxs[k], out)
        out = jnp.where(lane == TOP_K + k, rank, out)
        out = jnp.where(lane == 2 * TOP_K + k, exps[k] / denom, out)
    r_ref[...] = out


def _router(x, g, router_w, router_b):
    t, d = x.shape
    n_exp = router_w.shape[1]
    tm = _tile(t, 512)
    return pl.pallas_call(
        _router_kernel,
        grid=(t // tm,),
        in_specs=[
            pl.BlockSpec((tm, d), lambda i: (i, 0)),
            pl.BlockSpec((1, d), lambda i: (0, 0)),
            pl.BlockSpec((d, n_exp), lambda i: (0, 0)),
            pl.BlockSpec((1, n_exp), lambda i: (0, 0)),
        ],
        out_specs=[
            pl.BlockSpec((tm, d // 2), lambda i: (i, 0)),
            pl.BlockSpec((tm, V7X_LANES), lambda i: (i, 0)),
            pl.BlockSpec((1, n_exp), lambda i: (0, 0)),
        ],
        out_shape=[
            jax.ShapeDtypeStruct((t, d // 2), U32),
            jax.ShapeDtypeStruct((t, V7X_LANES), F32),
            jax.ShapeDtypeStruct((1, n_exp), F32),
        ],
        scratch_shapes=[pltpu.VMEM((1, n_exp), F32)],
        compiler_params=_params(("arbitrary",)),
        name="moe_router",
    )(x, g.reshape(1, d), router_w, router_b.reshape(1, n_exp))


def _dispatch_kernel(dest_ref, hp_ref, xs_init_ref, xs_ref, sem):
    del xs_init_ref
    tm = hp_ref.shape[0]

    def row_copy(r, k):
        return pltpu.make_async_copy(hp_ref.at[pl.ds(r, 1), :],
                                     xs_ref.at[pl.ds(dest_ref[r * TOP_K + k], 1), :], sem)

    def issue(r, carry):
        for k in range(TOP_K):
            row_copy(r, k).start(priority=k % 2)
        return carry

    def drain(r, carry):
        for k in range(TOP_K):
            row_copy(r, k).wait()
        return carry

    lax.fori_loop(0, tm, issue, 0)
    lax.fori_loop(0, tm, drain, 0)


def _dispatch(hp, dest_flat, n_slots):
    t, dw = hp.shape
    tm = _tile(t, 256)
    xs_init = jnp.zeros((n_slots, dw), U32)
    return pl.pallas_call(
        _dispatch_kernel,
        grid=(t // tm,),
        in_specs=[
            pl.BlockSpec((tm * TOP_K,), lambda i: (i,), memory_space=pltpu.SMEM),
            pl.BlockSpec((tm, dw), lambda i: (i, 0)),
            pl.BlockSpec(memory_space=pl.ANY),
        ],
        out_specs=pl.BlockSpec(memory_space=pl.ANY),
        out_shape=jax.ShapeDtypeStruct((n_slots, dw), U32),
        scratch_shapes=[pltpu.SemaphoreType.DMA(())],
        input_output_aliases={2: 0},
        compiler_params=_params(("arbitrary",)),
        name="moe_dispatch",
    )(dest_flat, hp, xs_init)


def _experts_kernel(te_ref, ns_ref, nu_ref, xs_ref, wgu_ref, wd_ref, bgu_ref, bd_ref, y_ref, x_ref, *, sub_rows):
    del te_ref, nu_ref
    i, c = pl.program_id(0), pl.program_id(1)
    n_sub = ns_ref[i]
    bm = y_ref.shape[0]
    half_words = xs_ref.shape[1]
    cw = wgu_ref.shape[1]
    hw = cw // 2

    @pl.when(c == 0)
    def _():
        words = xs_ref[...]
        x_ref[:, :half_words] = pltpu.bitcast(words << 16, F32).astype(BF16)
        x_ref[:, half_words:] = pltpu.bitcast(words & jnp.uint32(HI16), F32).astype(BF16)
        for sb in range(bm // sub_rows):
            y_ref[sb * sub_rows:(sb + 1) * sub_rows, :] = jnp.broadcast_to(
                jnp.where(sb < n_sub, bd_ref[...], 0.0), (sub_rows, y_ref.shape[1]))

    def token_rows(m_rows):
        wgu = wgu_ref[...].astype(BF16)
        lo = pltpu.bitcast(wd_ref[:hw // 2, :].astype(BF16).astype(F32), U32) >> 16
        hi = pltpu.bitcast(wd_ref[hw // 2:, :].astype(BF16).astype(F32), U32) & jnp.uint32(HI16)
        wdi = pltpu.bitcast(lo | hi, BF16)
        rows = slice(0, m_rows)
        even = (lax.broadcasted_iota(I32, (m_rows, V7X_LANES), 1) & 1) == 0
        gu = jnp.dot(x_ref[rows, :], wgu, preferred_element_type=F32) + bgu_ref[...]
        gs = jnp.minimum(gu, SWIGLU_LIMIT)
        gate = gs * jax.nn.sigmoid(SWIGLU_ALPHA * gs)
        up = jnp.clip(gu, -SWIGLU_LIMIT, SWIGLU_LIMIT) + 1.0
        prod = []
        for v in range(cw // V7X_LANES):
            cols = slice(v * V7X_LANES, (v + 1) * V7X_LANES)
            prod.append(gate[:, cols] * pltpu.roll(up[:, cols], V7X_LANES - 1, 1))
        nv = hw // V7X_LANES
        act = jnp.concatenate(
            [jnp.where(even, prod[v], pltpu.roll(prod[nv + v], 1, 1)) for v in range(nv)], axis=1)
        y_ref[rows, :] += jnp.dot(act.astype(BF16), wdi, preferred_element_type=F32)

    for k in range(1, bm // sub_rows + 1):
        @pl.when(n_sub == k)
        def _(k=k):
            token_rows(k * sub_rows)


def _experts(xs, tile_expert, tile_subs, n_used, w_gu, w_d, b_gu, b_d, *, layer, bm, sub_rows):
    n_slots, dw = xs.shape
    depth, n_exp, d, f2 = w_gu.shape
    cw = _tile(f2, MOE_GU_CHUNK)
    assert cw % (2 * V7X_LANES) == 0
    nt, nc = n_slots // bm, f2 // cw

    def tile_idx(i, nu):
        return jnp.minimum(i, nu[0] - 1)

    def chunk_idx(i, c, nu):
        return jnp.where(i < nu[0], c, nc - 1)

    def expert(i, te, nu):
        return te[tile_idx(i, nu)]

    grid_spec = pltpu.PrefetchScalarGridSpec(
        num_scalar_prefetch=3,
        grid=(nt, nc),
        in_specs=[
            pl.BlockSpec((bm, dw), lambda i, c, te, ns, nu: (tile_idx(i, nu), 0)),
            pl.BlockSpec((None, None, d, cw),
                         lambda i, c, te, ns, nu: (layer, expert(i, te, nu), 0, chunk_idx(i, c, nu))),
            pl.BlockSpec((None, None, cw // 2, d),
                         lambda i, c, te, ns, nu: (layer, expert(i, te, nu), chunk_idx(i, c, nu), 0)),
            pl.BlockSpec((None, None, 1, cw),
                         lambda i, c, te, ns, nu: (layer, expert(i, te, nu), 0, chunk_idx(i, c, nu))),
            pl.BlockSpec((None, None, 1, d), lambda i, c, te, ns, nu: (layer, expert(i, te, nu), 0, 0)),
        ],
        out_specs=pl.BlockSpec((bm, d), lambda i, c, te, ns, nu: (i, 0)),
        scratch_shapes=[pltpu.VMEM((bm, d), BF16)],
    )
    return pl.pallas_call(
        functools.partial(_experts_kernel, sub_rows=sub_rows),
        grid_spec=grid_spec,
        out_shape=jax.ShapeDtypeStruct((n_slots, d), F32),
        compiler_params=_params(("arbitrary", "arbitrary"), VMEM_LIMIT_EXPERTS_BYTES),
        name="moe_experts",
    )(tile_expert, tile_subs, n_used, xs, w_gu, w_d,
      b_gu.reshape(depth, n_exp, 1, f2), b_d.reshape(depth, n_exp, 1, d))


def _combine_kernel(dest_ref, x_ref, r_ref, y_ref, *rest, final_norm):
    if final_norm:
        g_ref, o_ref, buf_ref, sem = rest
    else:
        o_ref, buf_ref, sem = rest
    tm = x_ref.shape[0]
    n_parts = sem.shape[0]
    part = tm // n_parts

    def row_copy(r, k, p):
        return pltpu.make_async_copy(y_ref.at[pl.ds(dest_ref[r * TOP_K + k], 1), :],
                                     buf_ref.at[k, pl.ds(r, 1), :], sem.at[p])

    for p in range(n_parts):
        def issue(r, carry, p=p):
            for k in range(TOP_K):
                row_copy(r, k, p).start(priority=k % 2)
            return carry
        lax.fori_loop(p * part, (p + 1) * part, issue, 0)

    for p in range(n_parts):
        def drain(r, carry, p=p):
            for k in range(TOP_K):
                row_copy(r, k, p).wait()
            return carry
        lax.fori_loop(p * part, (p + 1) * part, drain, 0)

        rows = slice(p * part, (p + 1) * part)
        acc = x_ref[rows, :]
        for k in range(TOP_K):
            acc = acc + r_ref[rows, 2 * TOP_K + k:2 * TOP_K + k + 1] * buf_ref[k, rows, :]
        if final_norm:
            acc = acc * _rms_scale(acc) * g_ref[...]
        o_ref[rows, :] = acc


def _combine(x, r, y, dest_flat, final_g=None):
    t, d = x.shape
    tm = _tile(t, 256)
    in_specs = [
        pl.BlockSpec((tm * TOP_K,), lambda i: (i,), memory_space=pltpu.SMEM),
        pl.BlockSpec((tm, d), lambda i: (i, 0)),
        pl.BlockSpec((tm, V7X_LANES), lambda i: (i, 0)),
        pl.BlockSpec(memory_space=pl.ANY),
    ]
    args = [dest_flat, x, r, y]
    if final_g is not None:
        in_specs.append(pl.BlockSpec((1, d), lambda i: (0, 0)))
        args.append(final_g.reshape(1, d))
    return pl.pallas_call(
        functools.partial(_combine_kernel, final_norm=final_g is not None),
        grid=(t // tm,),
        in_specs=in_specs,
        out_specs=pl.BlockSpec((tm, d), lambda i: (i, 0)),
        out_shape=jax.ShapeDtypeStruct((t, d), F32),
        scratch_shapes=[pltpu.VMEM((TOP_K, tm, d), F32), pltpu.SemaphoreType.DMA((COMBINE_PARTS,))],
        compiler_params=_params(("arbitrary",)),
        name="moe_combine",
    )(*args)


def _moe(x, norm_g, router_w, router_b, w_gate_up, b_gate_up, w_down, b_down, *, layer, final_g=None):
    t, d = x.shape
    n_exp = router_w.shape[1]
    bm = MOE_ROW_TILE
    sub_rows = _tile(bm, MOE_SUB_ROWS)
    n_tiles = t * TOP_K // bm + n_exp
    hp, r, cnt = _router(x, norm_g, router_w, router_b)

    top_i = r[:, :TOP_K].astype(I32)
    rank = r[:, TOP_K:2 * TOP_K].astype(I32)
    counts = cnt[0].astype(I32)
    exp_tiles = (counts + bm - 1) // bm
    tile_end = jnp.cumsum(exp_tiles)
    tile_start = tile_end - exp_tiles
    per_tile = jnp.maximum((counts + exp_tiles * sub_rows - 1) // jnp.maximum(exp_tiles * sub_rows, 1), 1) * sub_rows
    tok_fill = per_tile[top_i]
    dest_flat = ((tile_start[top_i] + rank // tok_fill) * bm + rank % tok_fill).reshape(-1)
    n_used = tile_end[-1:].astype(I32)
    tile_ids = jnp.arange(n_tiles, dtype=I32)
    tile_expert = jnp.minimum(jnp.sum(tile_end[None, :] <= tile_ids[:, None], axis=1), n_exp - 1).astype(I32)
    tile_rows = jnp.clip(counts[tile_expert] - (tile_ids - tile_start[tile_expert]) * per_tile[tile_expert],
                         0, per_tile[tile_expert])
    tile_subs = jnp.where(tile_ids < n_used[0], (tile_rows + sub_rows - 1) // sub_rows, 0).astype(I32)

    xs = _dispatch(hp, dest_flat, n_tiles * bm)
    y = _experts(xs, tile_expert, tile_subs, n_used, w_gate_up, w_down, b_gate_up, b_down,
                 layer=layer, bm=bm, sub_rows=sub_rows)
    return _combine(x, r, y, dest_flat, final_g)


def _rope_tables(seq, dh, rot_dim, q_scale):
    half = rot_dim // 2
    inv_freq = ROPE_THETA ** (-jnp.arange(0, rot_dim, 2, dtype=F32) / rot_dim)
    ang = jnp.arange(seq, dtype=F32)[:, None] * inv_freq[None, :]
    cos, sin = jnp.cos(ang), jnp.sin(ang)
    ones = jnp.ones((seq, dh - rot_dim), F32)
    zeros_rest = jnp.zeros((seq, dh - rot_dim), F32)
    zeros_half = jnp.zeros((seq, half), F32)
    c = jnp.concatenate([cos, cos, ones], axis=1)
    s1 = jnp.concatenate([zeros_half, sin, zeros_rest], axis=1)
    s2 = jnp.concatenate([-sin, zeros_half, zeros_rest], axis=1)
    k_tab = jnp.stack([jnp.tile(a, (1, 2)) for a in (c, s1, s2)])
    v_tab = jnp.stack([jnp.ones_like(k_tab[0]), jnp.zeros_like(k_tab[0]), jnp.zeros_like(k_tab[0])])
    return jnp.stack([k_tab * q_scale, k_tab, v_tab])


def _gmlp_layer(x, norm_g, w_in, ln_g, ln_b, w_s, b_s, w_out):
    z = _norm_matmul(x, norm_g, w_in.astype(BF16))
    y = _gmlp_gate(z, ln_g, ln_b, w_s, b_s)
    return _matmul_residual(y, w_out.astype(BF16), x)


def _attn_layer(x, norm_g, w_qkv, lq1, lk1, lq2, lk2, subln_g, w_o, lambda_init, *, batch, seq):
    d = x.shape[1]
    dh = lq1.shape[0]
    n_heads = d // (2 * dh)
    rot_dim = dh // 4
    tab = _rope_tables(seq, dh, rot_dim, dh ** -0.5)
    qkv = _norm_matmul(x, norm_g, w_qkv.astype(BF16), rope_tab=tab, seq=seq, rot_half=rot_dim // 2)
    lam = jnp.exp(jnp.sum(lq1 * lk1)) - jnp.exp(jnp.sum(lq2 * lk2)) + lambda_init
    o = _diff_attention(qkv, lam.astype(F32), subln_g, batch=batch, seq=seq, n_heads=n_heads,
                        out_scale=1.0 - lambda_init)
    return _matmul_residual(o, w_o.astype(BF16), x)


def kernel(x, attn_norm_g, ffn_norm_g, final_norm_g, a_w_in, a_ln_g, a_ln_b, a_w_s, a_b_s, a_w_out,
           b_w_qkv, b_lambda_q1, b_lambda_k1, b_lambda_q2, b_lambda_k2, b_subln_g, b_w_o,
           router_w, router_b, w_gate_up, b_gate_up, w_down, b_down):
    batch, seq, d = x.shape
    depth = attn_norm_g.shape[0]
    n_mixers = 2
    h = x.reshape(batch * seq, d)
    for i in range(depth):
        j = i // n_mixers
        if i % n_mixers == 0:
            h = _gmlp_layer(h, attn_norm_g[i], a_w_in[j], a_ln_g[j], a_ln_b[j], a_w_s[j], a_b_s[j], a_w_out[j])
        else:
            lambda_init = 0.8 - 0.6 * math.exp(-0.3 * i)
            h = _attn_layer(h, attn_norm_g[i], b_w_qkv[j], b_lambda_q1[j], b_lambda_k1[j], b_lambda_q2[j],
                            b_lambda_k2[j], b_subln_g[j], b_w_o[j], lambda_init, batch=batch, seq=seq)
        h = _moe(h, ffn_norm_g[i], router_w[i], router_b[i], w_gate_up, b_gate_up, w_down, b_down,
                 layer=i, final_g=final_norm_g if i == depth - 1 else None)
    return h.reshape(batch, seq, d)
```

```python
import functools
import math

import jax
import jax.numpy as jnp
from jax import lax
from jax.experimental import pallas as pl
from jax.experimental.pallas import tpu as pltpu

F32 = jnp.float32
BF16 = jnp.bfloat16
I32 = jnp.int32
U32 = jnp.uint32

EPS = 1e-5
TOP_K = 4
ROPE_THETA = 500000.0
SWIGLU_LIMIT = 7.0
SWIGLU_ALPHA = 1.702

V7X_LANES = 128
V7X_VMEM_BYTES = 64 * 1024 * 1024
VMEM_LIMIT_BYTES = V7X_VMEM_BYTES * 7 // 8
VMEM_LIMIT_EXPERTS_BYTES = V7X_VMEM_BYTES * 15 // 16
HI16 = 0xFFFF0000

ATTN_Q_TILE = 512
COMBINE_PARTS = 1
MOE_ROW_TILE = 1024
MOE_SUB_ROWS = 128
MOE_GU_CHUNK = 1024


def _tile(dim, want):
    t = min(dim, want)
    while dim % t:
        t //= 2
    return t


def _params(semantics, vmem_limit_bytes=VMEM_LIMIT_BYTES):
    return pltpu.CompilerParams(dimension_semantics=semantics, vmem_limit_bytes=vmem_limit_bytes)


def _rms_scale(x):
    return lax.rsqrt(jnp.mean(x * x, axis=-1, keepdims=True) + EPS)


def _norm_matmul_gelu_kernel(x_ref, g_ref, w_ref, o_ref, xn_ref):
    @pl.when(pl.program_id(1) == 0)
    def _():
        x = x_ref[...]
        xn_ref[...] = (x * _rms_scale(x) * g_ref[...]).astype(BF16)

    y = jnp.dot(xn_ref[...], w_ref[...], preferred_element_type=F32)
    o_ref[...] = jax.nn.gelu(y).astype(o_ref.dtype)


def _norm_matmul_rope_kernel(x_ref, g_ref, w_ref, tab_ref, o_ref, xn_ref, *, shift):
    @pl.when(pl.program_id(1) == 0)
    def _():
        x = x_ref[...]
        xn_ref[...] = (x * _rms_scale(x) * g_ref[...]).astype(BF16)

    y = jnp.dot(xn_ref[...], w_ref[...], preferred_element_type=F32)
    c, s1, s2 = tab_ref[0], tab_ref[1], tab_ref[2]
    for h in range(y.shape[1] // V7X_LANES):
        cols = slice(h * V7X_LANES, (h + 1) * V7X_LANES)
        yh = y[:, cols]
        r = (yh * c + pltpu.roll(yh, shift, 1) * s1
             + pltpu.roll(yh, V7X_LANES - shift, 1) * s2)
        o_ref[:, cols] = r.astype(o_ref.dtype)


def _norm_matmul(x, g, w, *, rope_tab=None, seq=None, rot_half=0):
    t, k = x.shape
    n = w.shape[1]
    tm = _tile(t if seq is None else seq, 1024)
    tn = _tile(n if rope_tab is None else n // 3, 1024)
    grid = (t // tm, n // tn)
    in_specs = [
        pl.BlockSpec((tm, k), lambda i, j: (i, 0)),
        pl.BlockSpec((1, k), lambda i, j: (0, 0)),
        pl.BlockSpec((k, tn), lambda i, j: (0, j)),
    ]
    args = [x, g.reshape(1, k), w]
    if rope_tab is None:
        body = _norm_matmul_gelu_kernel
    else:
        assert n % (3 * tn) == 0 and tn % V7X_LANES == 0
        section_tiles = n // (3 * tn)
        s_tiles = seq // tm
        in_specs.append(pl.BlockSpec(
            (None, 3, tm, V7X_LANES), lambda i, j: (j // section_tiles, 0, i % s_tiles, 0)))
        args.append(rope_tab)
        body = functools.partial(_norm_matmul_rope_kernel, shift=rot_half)
    return pl.pallas_call(
        body,
        grid=grid,
        in_specs=in_specs,
        out_specs=pl.BlockSpec((tm, tn), lambda i, j: (i, j)),
        out_shape=jax.ShapeDtypeStruct((t, n), BF16),
        scratch_shapes=[pltpu.VMEM((tm, k), BF16)],
        compiler_params=_params(("parallel", "arbitrary")),
        name="norm_matmul_gelu" if rope_tab is None else "norm_matmul_rope",
    )(*args)


def _matmul_residual_kernel(a_ref, w_ref, r_ref, o_ref):
    o_ref[...] = r_ref[...] + jnp.dot(a_ref[...], w_ref[...], preferred_element_type=F32)


def _matmul_residual(a, w, res):
    t, k = a.shape
    n = w.shape[1]
    tm, tn = _tile(t, 512), _tile(n, 2048)
    return pl.pallas_call(
        _matmul_residual_kernel,
        grid=(t // tm, n // tn),
        in_specs=[
            pl.BlockSpec((tm, k), lambda i, j: (i, 0)),
            pl.BlockSpec((k, tn), lambda i, j: (0, j)),
            pl.BlockSpec((tm, tn), lambda i, j: (i, j)),
        ],
        out_specs=pl.BlockSpec((tm, tn), lambda i, j: (i, j)),
        out_shape=jax.ShapeDtypeStruct((t, n), F32),
        compiler_params=_params(("parallel", "parallel")),
        name="matmul_residual",
    )(a, w, res)


def _gmlp_gate_kernel(u_ref, v_ref, lng_ref, lnb_ref, ws_ref, bst_ref, y_ref, vn_ref):
    n_groups, chunk, _ = ws_ref.shape
    tm, width = u_ref.shape
    gd = width // n_groups
    v = v_ref[...].astype(F32)
    vc = v - jnp.mean(v, axis=-1, keepdims=True)
    inv = lax.rsqrt(jnp.mean(vc * vc, axis=-1, keepdims=True) + EPS)
    vn_ref[...] = (vc * inv * lng_ref[...] + lnb_ref[...]).astype(BF16)
    causal = (lax.broadcasted_iota(I32, (chunk, chunk), 0) >= lax.broadcasted_iota(I32, (chunk, chunk), 1))
    for g in range(n_groups):
        w = jnp.where(causal, ws_ref[g], 0.0).astype(BF16)
        bias = bst_ref[:, g:g + 1]
        cols = slice(g * gd, (g + 1) * gd)
        for ci in range(tm // chunk):
            rows = slice(ci * chunk, (ci + 1) * chunk)
            sv = jnp.dot(w, vn_ref[rows, cols], preferred_element_type=F32) + bias
            y_ref[rows, cols] = (u_ref[rows, cols].astype(F32) * sv).astype(y_ref.dtype)


def _gmlp_gate(z, ln_g, ln_b, w_s, b_s):
    t, w2 = z.shape
    width = w2 // 2
    n_groups, chunk, _ = w_s.shape
    tm = _tile(t, 4 * chunk)
    return pl.pallas_call(
        _gmlp_gate_kernel,
        grid=(t // tm,),
        in_specs=[
            pl.BlockSpec((tm, width), lambda i: (i, 0)),
            pl.BlockSpec((tm, width), lambda i: (i, 1)),
            pl.BlockSpec((1, width), lambda i: (0, 0)),
            pl.BlockSpec((1, width), lambda i: (0, 0)),
            pl.BlockSpec((n_groups, chunk, chunk), lambda i: (0, 0, 0)),
            pl.BlockSpec((chunk, n_groups), lambda i: (0, 0)),
        ],
        out_specs=pl.BlockSpec((tm, width), lambda i: (i, 0)),
        out_shape=jax.ShapeDtypeStruct((t, width), BF16),
        scratch_shapes=[pltpu.VMEM((tm, width), BF16)],
        compiler_params=_params(("parallel",)),
        name="gmlp_gate",
    )(z, z, ln_g.reshape(1, width), ln_b.reshape(1, width), w_s, b_s.T)


def _diff_attn_kernel(lam_ref, q_ref, k_ref, v_ref, g_ref, o_ref, qs_ref, sw_ref, sn_ref, m_ref, acc_ref, *,
                      out_scale):
    qi = pl.program_id(2)
    tq, dh2 = q_ref.shape
    half = dh2 // 2
    n_wide = qi // 2
    has_odd = qi % 2 == 1

    q = q_ref[...]
    lane = lax.broadcasted_iota(I32, q.shape, 1)
    qs_ref[:tq, :] = jnp.where(lane < half, q, jnp.zeros_like(q))
    qs_ref[tq:, :] = jnp.where(lane >= half, q, jnp.zeros_like(q))
    m_ref[...] = jnp.full_like(m_ref, -jnp.inf)
    acc_ref[...] = jnp.zeros_like(acc_ref)

    def kv_rows(index, width):
        return pl.ds(pl.multiple_of(index * width, width), width)

    def scores(s_ref, slot, index):
        width = s_ref.shape[2]
        s_ref[slot] = lax.dot_general(qs_ref[...], k_ref[kv_rows(index, width), :], (((1,), (1,)), ((), ())),
                                      preferred_element_type=F32)

    def softmax_pv(s_ref, slot, index, masked=False):
        width = s_ref.shape[2]
        v = v_ref[kv_rows(index, width), :]
        v1 = jnp.concatenate([v, jnp.ones_like(v)], axis=1)
        for r in range(2):
            rs = slice(r * tq, (r + 1) * tq)
            s = s_ref[slot, rs, :]
            if masked:
                row = lax.broadcasted_iota(I32, s.shape, 0)
                col = lax.broadcasted_iota(I32, s.shape, 1)
                s = jnp.where(col <= row, s, -jnp.inf)
            m_old = m_ref[rs, :]
            m_new = jnp.maximum(m_old, jnp.max(s, axis=-1, keepdims=True))
            p = jnp.exp((s - m_new).astype(BF16))
            alpha = jnp.exp(m_old - m_new)
            acc_ref[rs, :] = alpha * acc_ref[rs, :] + jnp.dot(p, v1, preferred_element_type=F32)
            m_ref[rs, :] = m_new

    @pl.when(n_wide > 0)
    def _():
        scores(sw_ref, 0, 0)

    def wide_step(w, carry):
        slot = w % 2
        softmax_pv(sw_ref, slot, w)
        scores(sw_ref, 1 - slot, w + 1)
        return carry

    lax.fori_loop(0, n_wide - 1, wide_step, 0)
    last_wide = n_wide - 1

    @pl.when(jnp.logical_and(n_wide > 0, has_odd))
    def _():
        scores(sn_ref, 0, qi - 1)
        softmax_pv(sw_ref, last_wide % 2, last_wide)
        scores(sn_ref, 1, qi)
        softmax_pv(sn_ref, 0, qi - 1)
        softmax_pv(sn_ref, 1, qi, masked=True)

    @pl.when(jnp.logical_and(n_wide > 0, jnp.logical_not(has_odd)))
    def _():
        scores(sn_ref, 0, qi)
        softmax_pv(sw_ref, last_wide % 2, last_wide)
        softmax_pv(sn_ref, 0, qi, masked=True)

    @pl.when(jnp.logical_and(n_wide == 0, has_odd))
    def _():
        scores(sn_ref, 0, qi - 1)
        scores(sn_ref, 1, qi)
        softmax_pv(sn_ref, 0, qi - 1)
        softmax_pv(sn_ref, 1, qi, masked=True)

    @pl.when(jnp.logical_and(n_wide == 0, jnp.logical_not(has_odd)))
    def _():
        scores(sn_ref, 0, qi)
        softmax_pv(sn_ref, 0, qi, masked=True)

    o = acc_ref[:, :dh2] / acc_ref[:, dh2:]
    o = o[:tq] - lam_ref[0] * o[tq:]
    o = o * _rms_scale(o) * g_ref[...] * out_scale
    o_ref[...] = o.astype(o_ref.dtype)


def _diff_attention(qkv, lam, subln_g, *, batch, seq, n_heads, out_scale):
    t = qkv.shape[0]
    dh2 = qkv.shape[1] // (3 * n_heads)
    tq = _tile(seq, ATTN_Q_TILE)
    nq = seq // tq
    return pl.pallas_call(
        functools.partial(_diff_attn_kernel, out_scale=out_scale),
        grid=(batch, n_heads, nq),
        in_specs=[
            pl.BlockSpec(memory_space=pltpu.SMEM),
            pl.BlockSpec((tq, dh2), lambda b, h, i: (b * nq + i, h)),
            pl.BlockSpec((seq, dh2), lambda b, h, i: (b, n_heads + h)),
            pl.BlockSpec((seq, dh2), lambda b, h, i: (b, 2 * n_heads + h)),
            pl.BlockSpec((1, dh2), lambda b, h, i: (0, 0)),
        ],
        out_specs=pl.BlockSpec((tq, dh2), lambda b, h, i: (b * nq + i, h)),
        out_shape=jax.ShapeDtypeStruct((t, n_heads * dh2), BF16),
        scratch_shapes=[
            pltpu.VMEM((2 * tq, dh2), BF16),
            pltpu.VMEM((2, 2 * tq, 2 * tq), F32),
            pltpu.VMEM((2, 2 * tq, tq), F32),
            pltpu.VMEM((2 * tq, 1), F32),
            pltpu.VMEM((2 * tq, 2 * dh2), F32),
        ],
        compiler_params=_params(("parallel", "parallel", "arbitrary")),
        name="diff_attention",
    )(lam.reshape(1), qkv, qkv, qkv, subln_g.reshape(1, dh2))


def _router_kernel(x_ref, g_ref, rw_ref, rb_ref, hp_ref, r_ref, cnt_ref, carry_ref):
    @pl.when(pl.program_id(0) == 0)
    def _():
        carry_ref[...] = jnp.zeros_like(carry_ref)

    x = x_ref[...]
    tm, d = x.shape
    n_exp = rw_ref.shape[1]
    h = x * _rms_scale(x) * g_ref[...]
    h_hi = h.astype(BF16)
    h_hi32 = h_hi.astype(F32)
    half = d // 2
    lo_word = pltpu.bitcast(h_hi32[:, :half], U32) >> 16
    hi_word = pltpu.bitcast(h_hi32[:, half:], U32) & jnp.uint32(HI16)
    hp_ref[...] = lo_word | hi_word

    h_lo = (h - h_hi32).astype(BF16)
    rw = rw_ref[...]
    w_hi = rw.astype(BF16)
    w_lo = (rw - w_hi.astype(F32)).astype(BF16)
    logits = (jnp.dot(h_hi, w_hi, preferred_element_type=F32)
              + jnp.dot(h_lo, w_hi, preferred_element_type=F32)
              + jnp.dot(h_hi, w_lo, preferred_element_type=F32)) + rb_ref[...]

    e_iota = lax.broadcasted_iota(I32, (tm, n_exp), 1).astype(F32)
    work = logits
    vals, onehots, idxs = [], [], []
    for _ in range(TOP_K):
        m = jnp.max(work, axis=-1, keepdims=True)
        idx = jnp.min(jnp.where(work == m, e_iota, float(n_exp)), axis=-1, keepdims=True)
        hit = e_iota == idx
        vals.append(m)
        idxs.append(idx)
        onehots.append(hit)
        work = jnp.where(hit, -jnp.inf, work)
    exps = [jnp.exp(v - vals[0]) for v in vals]
    denom = functools.reduce(lambda a, b: a + b, exps)

    chosen = functools.reduce(jnp.logical_or, onehots)
    chosen_f = jnp.where(chosen, 1.0, 0.0)
    earlier = (lax.broadcasted_iota(I32, (tm, tm), 0) > lax.broadcasted_iota(I32, (tm, tm), 1))
    before = jnp.dot(jnp.where(earlier, 1.0, 0.0).astype(BF16), chosen_f.astype(BF16),
                     preferred_element_type=F32) + carry_ref[...]
    carry_ref[...] = carry_ref[...] + jnp.sum(chosen_f, axis=0, keepdims=True)
    cnt_ref[...] = carry_ref[...]

    lane = lax.broadcasted_iota(I32, r_ref.shape, 1)
    out = jnp.zeros(r_ref.shape, F32)
    for k in range(TOP_K):
        rank = jnp.sum(jnp.where(onehots[k], before, 0.0), axis=-1, keepdims=True)
        out = jnp.where(lane == k, idxs[k], out)
        out = jnp.where(lane == TOP_K + k, rank, out)
        out = jnp.where(lane == 2 * TOP_K + k, exps[k] / denom, out)
    r_ref[...] = out


def _router(x, g, router_w, router_b):
    t, d = x.shape
    n_exp = router_w.shape[1]
    tm = _tile(t, 512)
    return pl.pallas_call(
        _router_kernel,
        grid=(t // tm,),
        in_specs=[
            pl.BlockSpec((tm, d), lambda i: (i, 0)),
            pl.BlockSpec((1, d), lambda i: (0, 0)),
            pl.BlockSpec((d, n_exp), lambda i: (0, 0)),
            pl.BlockSpec((1, n_exp), lambda i: (0, 0)),
        ],
        out_specs=[
            pl.BlockSpec((tm, d // 2), lambda i: (i, 0)),
            pl.BlockSpec((tm, V7X_LANES), lambda i: (i, 0)),
            pl.BlockSpec((1, n_exp), lambda i: (0, 0)),
        ],
        out_shape=[
            jax.ShapeDtypeStruct((t, d // 2), U32),
            jax.ShapeDtypeStruct((t, V7X_LANES), F32),
            jax.ShapeDtypeStruct((1, n_exp), F32),
        ],
        scratch_shapes=[pltpu.VMEM((1, n_exp), F32)],
        compiler_params=_params(("arbitrary",)),
        name="moe_router",
    )(x, g.reshape(1, d), router_w, router_b.reshape(1, n_exp))


def _dispatch_kernel(dest_ref, hp_ref, xs_init_ref, xs_ref, sem):
    del xs_init_ref
    tm = hp_ref.shape[0]

    def row_copy(r, k):
        return pltpu.make_async_copy(hp_ref.at[pl.ds(r, 1), :],
                                     xs_ref.at[pl.ds(dest_ref[r * TOP_K + k], 1), :], sem)

    def issue(r, carry):
        for k in range(TOP_K):
            row_copy(r, k).start(priority=k % 2)
        return carry

    def drain(r, carry):
        for k in range(TOP_K):
            row_copy(r, k).wait()
        return carry

    lax.fori_loop(0, tm, issue, 0)
    lax.fori_loop(0, tm, drain, 0)


def _dispatch(hp, dest_flat, n_slots):
    t, dw = hp.shape
    tm = _tile(t, 256)
    xs_init = jnp.zeros((n_slots, dw), U32)
    return pl.pallas_call(
        _dispatch_kernel,
        grid=(t // tm,),
        in_specs=[
            pl.BlockSpec((tm * TOP_K,), lambda i: (i,), memory_space=pltpu.SMEM),
            pl.BlockSpec((tm, dw), lambda i: (i, 0)),
            pl.BlockSpec(memory_space=pl.ANY),
        ],
        out_specs=pl.BlockSpec(memory_space=pl.ANY),
        out_shape=jax.ShapeDtypeStruct((n_slots, dw), U32),
        scratch_shapes=[pltpu.SemaphoreType.DMA(())],
        input_output_aliases={2: 0},
        compiler_params=_params(("arbitrary",)),
        name="moe_dispatch",
    )(dest_flat, hp, xs_init)


def _experts_kernel(te_ref, ns_ref, nu_ref, xs_ref, wgu_ref, wd_ref, bgu_ref, bd_ref, y_ref, x_ref, *, sub_rows):
    del te_ref, nu_ref
    i, c = pl.program_id(0), pl.program_id(1)
    n_sub = ns_ref[i]
    bm = y_ref.shape[0]
    half_words = xs_ref.shape[1]
    cw = wgu_ref.shape[1]
    hw = cw // 2

    @pl.when(c == 0)
    def _():
        words = xs_ref[...]
        x_ref[:, :half_words] = pltpu.bitcast(words << 16, F32).astype(BF16)
        x_ref[:, half_words:] = pltpu.bitcast(words & jnp.uint32(HI16), F32).astype(BF16)
        for sb in range(bm // sub_rows):
            y_ref[sb * sub_rows:(sb + 1) * sub_rows, :] = jnp.broadcast_to(
                jnp.where(sb < n_sub, bd_ref[...], 0.0), (sub_rows, y_ref.shape[1]))

    def token_rows(m_rows):
        wgu = wgu_ref[...].astype(BF16)
        lo = pltpu.bitcast(wd_ref[:hw // 2, :].astype(BF16).astype(F32), U32) >> 16
        hi = pltpu.bitcast(wd_ref[hw // 2:, :].astype(BF16).astype(F32), U32) & jnp.uint32(HI16)
        wdi = pltpu.bitcast(lo | hi, BF16)
        rows = slice(0, m_rows)
        even = (lax.broadcasted_iota(I32, (m_rows, V7X_LANES), 1) & 1) == 0
        gu = jnp.dot(x_ref[rows, :], wgu, preferred_element_type=F32) + bgu_ref[...]
        gs = jnp.minimum(gu, SWIGLU_LIMIT)
        gate = gs * jax.nn.sigmoid(SWIGLU_ALPHA * gs)
        up = jnp.clip(gu, -SWIGLU_LIMIT, SWIGLU_LIMIT) + 1.0
        prod = []
        for v in range(cw // V7X_LANES):
            cols = slice(v * V7X_LANES, (v + 1) * V7X_LANES)
            prod.append(gate[:, cols] * pltpu.roll(up[:, cols], V7X_LANES - 1, 1))
        nv = hw // V7X_LANES
        act = jnp.concatenate(
            [jnp.where(even, prod[v], pltpu.roll(prod[nv + v], 1, 1)) for v in range(nv)], axis=1)
        y_ref[rows, :] += jnp.dot(act.astype(BF16), wdi, preferred_element_type=F32)

    for k in range(1, bm // sub_rows + 1):
        @pl.when(n_sub == k)
        def _(k=k):
            token_rows(k * sub_rows)


def _experts(xs, tile_expert, tile_subs, n_used, w_gu, w_d, b_gu, b_d, *, layer, bm, sub_rows):
    n_slots, dw = xs.shape
    depth, n_exp, d, f2 = w_gu.shape
    cw = _tile(f2, MOE_GU_CHUNK)
    assert cw % (2 * V7X_LANES) == 0
    nt, nc = n_slots // bm, f2 // cw

    def tile_idx(i, nu):
        return jnp.minimum(i, nu[0] - 1)

    def chunk_idx(i, c, nu):
        return jnp.where(i < nu[0], c, nc - 1)

    def expert(i, te, nu):
        return te[tile_idx(i, nu)]

    grid_spec = pltpu.PrefetchScalarGridSpec(
        num_scalar_prefetch=3,
        grid=(nt, nc),
        in_specs=[
            pl.BlockSpec((bm, dw), lambda i, c, te, ns, nu: (tile_idx(i, nu), 0)),
            pl.BlockSpec((None, None, d, cw),
                         lambda i, c, te, ns, nu: (layer, expert(i, te, nu), 0, chunk_idx(i, c, nu))),
            pl.BlockSpec((None, None, cw // 2, d),
                         lambda i, c, te, ns, nu: (layer, expert(i, te, nu), chunk_idx(i, c, nu), 0)),
            pl.BlockSpec((None, None, 1, cw),
                         lambda i, c, te, ns, nu: (layer, expert(i, te, nu), 0, chunk_idx(i, c, nu))),
            pl.BlockSpec((None, None, 1, d), lambda i, c, te, ns, nu: (layer, expert(i, te, nu), 0, 0)),
        ],
        out_specs=pl.BlockSpec((bm, d), lambda i, c, te, ns, nu: (i, 0)),
        scratch_shapes=[pltpu.VMEM((bm, d), BF16)],
    )
    return pl.pallas_call(
        functools.partial(_experts_kernel, sub_rows=sub_rows),
        grid_spec=grid_spec,
        out_shape=jax.ShapeDtypeStruct((n_slots, d), F32),
        compiler_params=_params(("arbitrary", "arbitrary"), VMEM_LIMIT_EXPERTS_BYTES),
        name="moe_experts",
    )(tile_expert, tile_subs, n_used, xs, w_gu, w_d,
      b_gu.reshape(depth, n_exp, 1, f2), b_d.reshape(depth, n_exp, 1, d))


def _combine_kernel(dest_ref, x_ref, r_ref, y_ref, *rest, final_norm):
    if final_norm:
        g_ref, o_ref, buf_ref, sem = rest
    else:
        o_ref, buf_ref, sem = rest
    tm = x_ref.shape[0]
    n_parts = sem.shape[0]
    part = tm // n_parts

    def row_copy(r, k, p):
        return pltpu.make_async_copy(y_ref.at[pl.ds(dest_ref[r * TOP_K + k], 1), :],
                                     buf_ref.at[k, pl.ds(r, 1), :], sem.at[p])

    for p in range(n_parts):
        def issue(r, carry, p=p):
            for k in range(TOP_K):
                row_copy(r, k, p).start(priority=k % 2)
            return carry
        lax.fori_loop(p * part, (p + 1) * part, issue, 0)

    for p in range(n_parts):
        def drain(r, carry, p=p):
            for k in range(TOP_K):
                row_copy(r, k, p).wait()
            return carry
        lax.fori_loop(p * part, (p + 1) * part, drain, 0)

        rows = slice(p * part, (p + 1) * part)
        acc = x_ref[rows, :]
        for k in range(TOP_K):
            acc = acc + r_ref[rows, 2 * TOP_K + k:2 * TOP_K + k + 1] * buf_ref[k, rows, :]
        if final_norm:
            acc = acc * _rms_scale(acc) * g_ref[...]
        o_ref[rows, :] = acc


def _combine(x, r, y, dest_flat, final_g=None):
    t, d = x.shape
    tm = _tile(t, 256)
    in_specs = [
        pl.BlockSpec((tm * TOP_K,), lambda i: (i,), memory_space=pltpu.SMEM),
        pl.BlockSpec((tm, d), lambda i: (i, 0)),
        pl.BlockSpec((tm, V7X_LANES), lambda i: (i, 0)),
        pl.BlockSpec(memory_space=pl.ANY),
    ]
    args = [dest_flat, x, r, y]
    if final_g is not None:
        in_specs.append(pl.BlockSpec((1, d), lambda i: (0, 0)))
        args.append(final_g.reshape(1, d))
    return pl.pallas_call(
        functools.partial(_combine_kernel, final_norm=final_g is not None),
        grid=(t // tm,),
        in_specs=in_specs,
        out_specs=pl.BlockSpec((tm, d), lambda i: (i, 0)),
        out_shape=jax.ShapeDtypeStruct((t, d), F32),
        scratch_shapes=[pltpu.VMEM((TOP_K, tm, d), F32), pltpu.SemaphoreType.DMA((COMBINE_PARTS,))],
        compiler_params=_params(("arbitrary",)),
        name="moe_combine",
    )(*args)


def _moe(x, norm_g, router_w, router_b, w_gate_up, b_gate_up, w_down, b_down, *, layer, final_g=None):
    t, d = x.shape
    n_exp = router_w.shape[1]
    bm = MOE_ROW_TILE
    sub_rows = _tile(bm, MOE_SUB_ROWS)
    n_tiles = t * TOP_K // bm + n_exp
    hp, r, cnt = _router(x, norm_g, router_w, router_b)

    top_i = r[:, :TOP_K].astype(I32)
    rank = r[:, TOP_K:2 * TOP_K].astype(I32)
    counts = cnt[0].astype(I32)
    exp_tiles = (counts + bm - 1) // bm
    tile_end = jnp.cumsum(exp_tiles)
    tile_start = tile_end - exp_tiles
    per_tile = jnp.maximum((counts + exp_tiles * sub_rows - 1) // jnp.maximum(exp_tiles * sub_rows, 1), 1) * sub_rows
    tok_fill = per_tile[top_i]
    tok_tile = jnp.floor((rank.astype(F32) + 0.5) / tok_fill.astype(F32)).astype(I32)
    dest_flat = ((tile_start[top_i] + tok_tile) * bm + rank - tok_tile * tok_fill).reshape(-1)
    n_used = tile_end[-1:].astype(I32)
    tile_ids = jnp.arange(n_tiles, dtype=I32)
    tile_expert = jnp.minimum(jnp.sum(tile_end[None, :] <= tile_ids[:, None], axis=1), n_exp - 1).astype(I32)
    tile_rows = jnp.clip(counts[tile_expert] - (tile_ids - tile_start[tile_expert]) * per_tile[tile_expert],
                         0, per_tile[tile_expert])
    tile_subs = jnp.where(tile_ids < n_used[0], (tile_rows + sub_rows - 1) // sub_rows, 0).astype(I32)

    xs = _dispatch(hp, dest_flat, n_tiles * bm)
    y = _experts(xs, tile_expert, tile_subs, n_used, w_gate_up, w_down, b_gate_up, b_down,
                 layer=layer, bm=bm, sub_rows=sub_rows)
    return _combine(x, r, y, dest_flat, final_g)


def _rope_tables(seq, dh, rot_dim, q_scale):
    half = rot_dim // 2
    inv_freq = ROPE_THETA ** (-jnp.arange(0, rot_dim, 2, dtype=F32) / rot_dim)
    ang = jnp.arange(seq, dtype=F32)[:, None] * inv_freq[None, :]
    cos, sin = jnp.cos(ang), jnp.sin(ang)
    ones = jnp.ones((seq, dh - rot_dim), F32)
    zeros_rest = jnp.zeros((seq, dh - rot_dim), F32)
    zeros_half = jnp.zeros((seq, half), F32)
    c = jnp.concatenate([cos, cos, ones], axis=1)
    s1 = jnp.concatenate([zeros_half, sin, zeros_rest], axis=1)
    s2 = jnp.concatenate([-sin, zeros_half, zeros_rest], axis=1)
    k_tab = jnp.stack([jnp.tile(a, (1, 2)) for a in (c, s1, s2)])
    v_tab = jnp.stack([jnp.ones_like(k_tab[0]), jnp.zeros_like(k_tab[0]), jnp.zeros_like(k_tab[0])])
    return jnp.stack([k_tab * q_scale, k_tab, v_tab])


def _gmlp_layer(x, norm_g, w_in, ln_g, ln_b, w_s, b_s, w_out):
    z = _norm_matmul(x, norm_g, w_in.astype(BF16))
    y = _gmlp_gate(z, ln_g, ln_b, w_s, b_s)
    return _matmul_residual(y, w_out.astype(BF16), x)


def _attn_layer(x, norm_g, w_qkv, lq1, lk1, lq2, lk2, subln_g, w_o, lambda_init, *, batch, seq):
    d = x.shape[1]
    dh = lq1.shape[0]
    n_heads = d // (2 * dh)
    rot_dim = dh // 4
    tab = _rope_tables(seq, dh, rot_dim, dh ** -0.5)
    qkv = _norm_matmul(x, norm_g, w_qkv.astype(BF16), rope_tab=tab, seq=seq, rot_half=rot_dim // 2)
    lam = jnp.exp(jnp.sum(lq1 * lk1)) - jnp.exp(jnp.sum(lq2 * lk2)) + lambda_init
    o = _diff_attention(qkv, lam.astype(F32), subln_g, batch=batch, seq=seq, n_heads=n_heads,
                        out_scale=1.0 - lambda_init)
    return _matmul_residual(o, w_o.astype(BF16), x)


def kernel(x, attn_norm_g, ffn_norm_g, final_norm_g, a_w_in, a_ln_g, a_ln_b, a_w_s, a_b_s, a_w_out,
           b_w_qkv, b_lambda_q1, b_lambda_k1, b_lambda_q2, b_lambda_k2, b_subln_g, b_w_o,
           router_w, router_b, w_gate_up, b_gate_up, w_down, b_down):
    batch, seq, d = x.shape
    depth = attn_norm_g.shape[0]
    n_mixers = 2
    h = x.reshape(batch * seq, d)
    for i in range(depth):
        j = i // n_mixers
        if i % n_mixers == 0:
            h = _gmlp_layer(h, attn_norm_g[i], a_w_in[j], a_ln_g[j], a_ln_b[j], a_w_s[j], a_b_s[j], a_w_out[j])
        else:
            lambda_init = 0.8 - 0.6 * math.exp(-0.3 * i)
            h = _attn_layer(h, attn_norm_g[i], b_w_qkv[j], b_lambda_q1[j], b_lambda_k1[j], b_lambda_q2[j],
                            b_lambda_k2[j], b_subln_g[j], b_w_o[j], lambda_init, batch=batch, seq=seq)
        h = _moe(h, ffn_norm_g[i], router_w[i], router_b[i], w_gate_up, b_gate_up, w_down, b_down,
                 layer=i, final_g=final_norm_g if i == depth - 1 else None)
    return h.reshape(batch, seq, d)
```

```python
import functools
import math

import jax
import jax.numpy as jnp
from jax import lax
from jax.experimental import pallas as pl
from jax.experimental.pallas import tpu as pltpu

F32 = jnp.float32
BF16 = jnp.bfloat16
I32 = jnp.int32
U32 = jnp.uint32

EPS = 1e-5
TOP_K = 4
ROPE_THETA = 500000.0
SWIGLU_LIMIT = 7.0
SWIGLU_ALPHA = 1.702

V7X_LANES = 128
V7X_VMEM_BYTES = 64 * 1024 * 1024
VMEM_LIMIT_BYTES = V7X_VMEM_BYTES * 7 // 8
VMEM_LIMIT_EXPERTS_BYTES = V7X_VMEM_BYTES * 15 // 16
HI16 = 0xFFFF0000

ATTN_Q_TILE = 512
MOE_TOKEN_TILE = 512
COMBINE_SUM_ROWS = 128
MOE_ROW_TILE = 1024
MOE_SUB_ROWS = 128
MOE_GU_CHUNK = 1024


def _tile(dim, want):
    t = min(dim, want)
    while dim % t:
        t //= 2
    return t


def _params(semantics, vmem_limit_bytes=VMEM_LIMIT_BYTES):
    return pltpu.CompilerParams(dimension_semantics=semantics, vmem_limit_bytes=vmem_limit_bytes)


def _rms_scale(x):
    return lax.rsqrt(jnp.mean(x * x, axis=-1, keepdims=True) + EPS)


def _norm_matmul_gelu_kernel(x_ref, g_ref, w_ref, o_ref, xn_ref):
    @pl.when(pl.program_id(1) == 0)
    def _():
        x = x_ref[...]
        xn_ref[...] = (x * _rms_scale(x) * g_ref[...]).astype(BF16)

    y = jnp.dot(xn_ref[...], w_ref[...], preferred_element_type=F32)
    o_ref[...] = jax.nn.gelu(y).astype(o_ref.dtype)


def _norm_matmul_rope_kernel(x_ref, g_ref, w_ref, tab_ref, o_ref, xn_ref, *, shift):
    @pl.when(pl.program_id(1) == 0)
    def _():
        x = x_ref[...]
        xn_ref[...] = (x * _rms_scale(x) * g_ref[...]).astype(BF16)

    y = jnp.dot(xn_ref[...], w_ref[...], preferred_element_type=F32)
    c, s1, s2 = tab_ref[0], tab_ref[1], tab_ref[2]
    for h in range(y.shape[1] // V7X_LANES):
        cols = slice(h * V7X_LANES, (h + 1) * V7X_LANES)
        yh = y[:, cols]
        r = (yh * c + pltpu.roll(yh, shift, 1) * s1
             + pltpu.roll(yh, V7X_LANES - shift, 1) * s2)
        o_ref[:, cols] = r.astype(o_ref.dtype)


def _norm_matmul(x, g, w, *, rope_tab=None, seq=None, rot_half=0):
    t, k = x.shape
    n = w.shape[1]
    tm = _tile(t if seq is None else seq, 1024)
    tn = _tile(n if rope_tab is None else n // 3, 1024)
    grid = (t // tm, n // tn)
    in_specs = [
        pl.BlockSpec((tm, k), lambda i, j: (i, 0)),
        pl.BlockSpec((1, k), lambda i, j: (0, 0)),
        pl.BlockSpec((k, tn), lambda i, j: (0, j)),
    ]
    args = [x, g.reshape(1, k), w]
    if rope_tab is None:
        body = _norm_matmul_gelu_kernel
    else:
        assert n % (3 * tn) == 0 and tn % V7X_LANES == 0
        section_tiles = n // (3 * tn)
        s_tiles = seq // tm
        in_specs.append(pl.BlockSpec(
            (None, 3, tm, V7X_LANES), lambda i, j: (j // section_tiles, 0, i % s_tiles, 0)))
        args.append(rope_tab)
        body = functools.partial(_norm_matmul_rope_kernel, shift=rot_half)
    return pl.pallas_call(
        body,
        grid=grid,
        in_specs=in_specs,
        out_specs=pl.BlockSpec((tm, tn), lambda i, j: (i, j)),
        out_shape=jax.ShapeDtypeStruct((t, n), BF16),
        scratch_shapes=[pltpu.VMEM((tm, k), BF16)],
        compiler_params=_params(("parallel", "arbitrary")),
        name="norm_matmul_gelu" if rope_tab is None else "norm_matmul_rope",
    )(*args)


def _matmul_residual_kernel(a_ref, w_ref, r_ref, o_ref):
    o_ref[...] = r_ref[...] + jnp.dot(a_ref[...], w_ref[...], preferred_element_type=F32)


def _matmul_residual(a, w, res):
    t, k = a.shape
    n = w.shape[1]
    tm, tn = _tile(t, 512), _tile(n, 2048)
    return pl.pallas_call(
        _matmul_residual_kernel,
        grid=(t // tm, n // tn),
        in_specs=[
            pl.BlockSpec((tm, k), lambda i, j: (i, 0)),
            pl.BlockSpec((k, tn), lambda i, j: (0, j)),
            pl.BlockSpec((tm, tn), lambda i, j: (i, j)),
        ],
        out_specs=pl.BlockSpec((tm, tn), lambda i, j: (i, j)),
        out_shape=jax.ShapeDtypeStruct((t, n), F32),
        compiler_params=_params(("parallel", "parallel")),
        name="matmul_residual",
    )(a, w, res)


def _gmlp_gate_kernel(u_ref, v_ref, lng_ref, lnb_ref, ws_ref, bst_ref, y_ref, vn_ref):
    n_groups, chunk, _ = ws_ref.shape
    tm, width = u_ref.shape
    gd = width // n_groups
    v = v_ref[...].astype(F32)
    vc = v - jnp.mean(v, axis=-1, keepdims=True)
    inv = lax.rsqrt(jnp.mean(vc * vc, axis=-1, keepdims=True) + EPS)
    vn_ref[...] = (vc * inv * lng_ref[...] + lnb_ref[...]).astype(BF16)
    causal = (lax.broadcasted_iota(I32, (chunk, chunk), 0) >= lax.broadcasted_iota(I32, (chunk, chunk), 1))
    for g in range(n_groups):
        w = jnp.where(causal, ws_ref[g], 0.0).astype(BF16)
        bias = bst_ref[:, g:g + 1]
        cols = slice(g * gd, (g + 1) * gd)
        for ci in range(tm // chunk):
            rows = slice(ci * chunk, (ci + 1) * chunk)
            sv = jnp.dot(w, vn_ref[rows, cols], preferred_element_type=F32) + bias
            y_ref[rows, cols] = (u_ref[rows, cols].astype(F32) * sv).astype(y_ref.dtype)


def _gmlp_gate(z, ln_g, ln_b, w_s, b_s):
    t, w2 = z.shape
    width = w2 // 2
    n_groups, chunk, _ = w_s.shape
    tm = _tile(t, 4 * chunk)
    return pl.pallas_call(
        _gmlp_gate_kernel,
        grid=(t // tm,),
        in_specs=[
            pl.BlockSpec((tm, width), lambda i: (i, 0)),
            pl.BlockSpec((tm, width), lambda i: (i, 1)),
            pl.BlockSpec((1, width), lambda i: (0, 0)),
            pl.BlockSpec((1, width), lambda i: (0, 0)),
            pl.BlockSpec((n_groups, chunk, chunk), lambda i: (0, 0, 0)),
            pl.BlockSpec((chunk, n_groups), lambda i: (0, 0)),
        ],
        out_specs=pl.BlockSpec((tm, width), lambda i: (i, 0)),
        out_shape=jax.ShapeDtypeStruct((t, width), BF16),
        scratch_shapes=[pltpu.VMEM((tm, width), BF16)],
        compiler_params=_params(("parallel",)),
        name="gmlp_gate",
    )(z, z, ln_g.reshape(1, width), ln_b.reshape(1, width), w_s, b_s.T)


def _diff_attn_kernel(lam_ref, q_ref, k_ref, v_ref, g_ref, o_ref, qs_ref, sw_ref, sn_ref, m_ref, acc_ref, *,
                      out_scale):
    qi = pl.program_id(2)
    tq, dh2 = q_ref.shape
    half = dh2 // 2
    n_wide = qi // 2
    has_odd = qi % 2 == 1

    q = q_ref[...]
    lane = lax.broadcasted_iota(I32, q.shape, 1)
    qs_ref[:tq, :] = jnp.where(lane < half, q, jnp.zeros_like(q))
    qs_ref[tq:, :] = jnp.where(lane >= half, q, jnp.zeros_like(q))
    m_ref[...] = jnp.full_like(m_ref, -jnp.inf)
    acc_ref[...] = jnp.zeros_like(acc_ref)

    def kv_rows(index, width):
        return pl.ds(pl.multiple_of(index * width, width), width)

    def scores(s_ref, slot, index):
        width = s_ref.shape[2]
        s_ref[slot] = lax.dot_general(qs_ref[...], k_ref[kv_rows(index, width), :], (((1,), (1,)), ((), ())),
                                      preferred_element_type=F32)

    def softmax_pv(s_ref, slot, index, masked=False):
        width = s_ref.shape[2]
        v = v_ref[kv_rows(index, width), :]
        v1 = jnp.concatenate([v, jnp.ones_like(v)], axis=1)
        for r in range(2):
            rs = slice(r * tq, (r + 1) * tq)
            s = s_ref[slot, rs, :]
            if masked:
                row = lax.broadcasted_iota(I32, s.shape, 0)
                col = lax.broadcasted_iota(I32, s.shape, 1)
                s = jnp.where(col <= row, s, -jnp.inf)
            m_old = m_ref[rs, :]
            m_new = jnp.maximum(m_old, jnp.max(s, axis=-1, keepdims=True))
            p = jnp.exp((s - m_new).astype(BF16))
            alpha = jnp.exp(m_old - m_new)
            acc_ref[rs, :] = alpha * acc_ref[rs, :] + jnp.dot(p, v1, preferred_element_type=F32)
            m_ref[rs, :] = m_new

    @pl.when(n_wide > 0)
    def _():
        scores(sw_ref, 0, 0)

    def wide_step(w, carry):
        slot = w % 2
        softmax_pv(sw_ref, slot, w)
        scores(sw_ref, 1 - slot, w + 1)
        return carry

    lax.fori_loop(0, n_wide - 1, wide_step, 0)
    last_wide = n_wide - 1

    @pl.when(jnp.logical_and(n_wide > 0, has_odd))
    def _():
        scores(sn_ref, 0, qi - 1)
        softmax_pv(sw_ref, last_wide % 2, last_wide)
        scores(sn_ref, 1, qi)
        softmax_pv(sn_ref, 0, qi - 1)
        softmax_pv(sn_ref, 1, qi, masked=True)

    @pl.when(jnp.logical_and(n_wide > 0, jnp.logical_not(has_odd)))
    def _():
        scores(sn_ref, 0, qi)
        softmax_pv(sw_ref, last_wide % 2, last_wide)
        softmax_pv(sn_ref, 0, qi, masked=True)

    @pl.when(jnp.logical_and(n_wide == 0, has_odd))
    def _():
        scores(sn_ref, 0, qi - 1)
        scores(sn_ref, 1, qi)
        softmax_pv(sn_ref, 0, qi - 1)
        softmax_pv(sn_ref, 1, qi, masked=True)

    @pl.when(jnp.logical_and(n_wide == 0, jnp.logical_not(has_odd)))
    def _():
        scores(sn_ref, 0, qi)
        softmax_pv(sn_ref, 0, qi, masked=True)

    o = acc_ref[:, :dh2] / acc_ref[:, dh2:]
    o = o[:tq] - lam_ref[0] * o[tq:]
    o = o * _rms_scale(o) * g_ref[...] * out_scale
    o_ref[...] = o.astype(o_ref.dtype)


def _diff_attention(qkv, lam, subln_g, *, batch, seq, n_heads, out_scale):
    t = qkv.shape[0]
    dh2 = qkv.shape[1] // (3 * n_heads)
    tq = _tile(seq, ATTN_Q_TILE)
    nq = seq // tq
    return pl.pallas_call(
        functools.partial(_diff_attn_kernel, out_scale=out_scale),
        grid=(batch, n_heads, nq),
        in_specs=[
            pl.BlockSpec(memory_space=pltpu.SMEM),
            pl.BlockSpec((tq, dh2), lambda b, h, i: (b * nq + i, h)),
            pl.BlockSpec((seq, dh2), lambda b, h, i: (b, n_heads + h)),
            pl.BlockSpec((seq, dh2), lambda b, h, i: (b, 2 * n_heads + h)),
            pl.BlockSpec((1, dh2), lambda b, h, i: (0, 0)),
        ],
        out_specs=pl.BlockSpec((tq, dh2), lambda b, h, i: (b * nq + i, h)),
        out_shape=jax.ShapeDtypeStruct((t, n_heads * dh2), BF16),
        scratch_shapes=[
            pltpu.VMEM((2 * tq, dh2), BF16),
            pltpu.VMEM((2, 2 * tq, 2 * tq), F32),
            pltpu.VMEM((2, 2 * tq, tq), F32),
            pltpu.VMEM((2 * tq, 1), F32),
            pltpu.VMEM((2 * tq, 2 * dh2), F32),
        ],
        compiler_params=_params(("parallel", "parallel", "arbitrary")),
        name="diff_attention",
    )(lam.reshape(1), qkv, qkv, qkv, subln_g.reshape(1, dh2))


def _router_kernel(x_ref, g_ref, rw_ref, rb_ref, hp_ref, r_ref, cnt_ref, carry_ref):
    @pl.when(pl.program_id(0) == 0)
    def _():
        carry_ref[...] = jnp.zeros_like(carry_ref)

    x = x_ref[...]
    tm, d = x.shape
    n_exp = rw_ref.shape[1]
    h = x * _rms_scale(x) * g_ref[...]
    h_hi = h.astype(BF16)
    h_hi32 = h_hi.astype(F32)
    half = d // 2
    lo_word = pltpu.bitcast(h_hi32[:, :half], U32) >> 16
    hi_word = pltpu.bitcast(h_hi32[:, half:], U32) & jnp.uint32(HI16)
    hp_ref[...] = lo_word | hi_word

    h_lo = (h - h_hi32).astype(BF16)
    rw = rw_ref[...]
    w_hi = rw.astype(BF16)
    w_lo = (rw - w_hi.astype(F32)).astype(BF16)
    logits = (jnp.dot(h_hi, w_hi, preferred_element_type=F32)
              + jnp.dot(h_lo, w_hi, preferred_element_type=F32)
              + jnp.dot(h_hi, w_lo, preferred_element_type=F32)) + rb_ref[...]

    e_iota = lax.broadcasted_iota(I32, (tm, n_exp), 1).astype(F32)
    work = logits
    vals, onehots, idxs = [], [], []
    for _ in range(TOP_K):
        m = jnp.max(work, axis=-1, keepdims=True)
        idx = jnp.min(jnp.where(work == m, e_iota, float(n_exp)), axis=-1, keepdims=True)
        hit = e_iota == idx
        vals.append(m)
        idxs.append(idx)
        onehots.append(hit)
        work = jnp.where(hit, -jnp.inf, work)
    exps = [jnp.exp(v - vals[0]) for v in vals]
    denom = functools.reduce(lambda a, b: a + b, exps)

    chosen = functools.reduce(jnp.logical_or, onehots)
    chosen_f = jnp.where(chosen, 1.0, 0.0)
    earlier = (lax.broadcasted_iota(I32, (tm, tm), 0) > lax.broadcasted_iota(I32, (tm, tm), 1))
    before = jnp.dot(jnp.where(earlier, 1.0, 0.0).astype(BF16), chosen_f.astype(BF16),
                     preferred_element_type=F32) + carry_ref[...]
    carry_ref[...] = carry_ref[...] + jnp.sum(chosen_f, axis=0, keepdims=True)
    cnt_ref[...] = carry_ref[...]

    lane = lax.broadcasted_iota(I32, r_ref.shape, 1)
    out = jnp.zeros(r_ref.shape, F32)
    for k in range(TOP_K):
        rank = jnp.sum(jnp.where(onehots[k], before, 0.0), axis=-1, keepdims=True)
        out = jnp.where(lane == k, idxs[k], out)
        out = jnp.where(lane == TOP_K + k, rank, out)
        out = jnp.where(lane == 2 * TOP_K + k, exps[k] / denom, out)
    r_ref[...] = out


def _router(x, g, router_w, router_b):
    t, d = x.shape
    n_exp = router_w.shape[1]
    tm = _tile(t, 512)
    return pl.pallas_call(
        _router_kernel,
        grid=(t // tm,),
        in_specs=[
            pl.BlockSpec((tm, d), lambda i: (i, 0)),
            pl.BlockSpec((1, d), lambda i: (0, 0)),
            pl.BlockSpec((d, n_exp), lambda i: (0, 0)),
            pl.BlockSpec((1, n_exp), lambda i: (0, 0)),
        ],
        out_specs=[
            pl.BlockSpec((tm, d // 2), lambda i: (i, 0)),
            pl.BlockSpec((tm, V7X_LANES), lambda i: (i, 0)),
            pl.BlockSpec((1, n_exp), lambda i: (0, 0)),
        ],
        out_shape=[
            jax.ShapeDtypeStruct((t, d // 2), U32),
            jax.ShapeDtypeStruct((t, V7X_LANES), F32),
            jax.ShapeDtypeStruct((1, n_exp), F32),
        ],
        scratch_shapes=[pltpu.VMEM((1, n_exp), F32)],
        compiler_params=_params(("arbitrary",)),
        name="moe_router",
    )(x, g.reshape(1, d), router_w, router_b.reshape(1, n_exp))


def _dispatch_kernel(dest_ref, hp_ref, xs_init_ref, xs_ref, sem):
    del xs_init_ref
    tm = hp_ref.shape[0]

    def row_copy(r, k):
        return pltpu.make_async_copy(hp_ref.at[pl.ds(r, 1), :],
                                     xs_ref.at[pl.ds(dest_ref[r * TOP_K + k], 1), :], sem)

    def issue(r, carry):
        for k in range(TOP_K):
            row_copy(r, k).start(priority=k % 2)
        return carry

    def drain(r, carry):
        for k in range(TOP_K):
            row_copy(r, k).wait()
        return carry

    lax.fori_loop(0, tm, issue, 0)
    lax.fori_loop(0, tm, drain, 0)


def _dispatch(hp, dest_flat, n_slots):
    t, dw = hp.shape
    tm = _tile(t, MOE_TOKEN_TILE)
    xs_init = jnp.zeros((n_slots, dw), U32)
    return pl.pallas_call(
        _dispatch_kernel,
        grid=(t // tm,),
        in_specs=[
            pl.BlockSpec((tm * TOP_K,), lambda i: (i,), memory_space=pltpu.SMEM),
            pl.BlockSpec((tm, dw), lambda i: (i, 0)),
            pl.BlockSpec(memory_space=pl.ANY),
        ],
        out_specs=pl.BlockSpec(memory_space=pl.ANY),
        out_shape=jax.ShapeDtypeStruct((n_slots, dw), U32),
        scratch_shapes=[pltpu.SemaphoreType.DMA(())],
        input_output_aliases={2: 0},
        compiler_params=_params(("arbitrary",)),
        name="moe_dispatch",
    )(dest_flat, hp, xs_init)


def _experts_kernel(te_ref, ns_ref, nu_ref, xs_ref, wgu_ref, wd_ref, bgu_ref, bd_ref, y_ref, x_ref, *, sub_rows):
    del te_ref, nu_ref
    i, c = pl.program_id(0), pl.program_id(1)
    n_sub = ns_ref[i]
    bm = y_ref.shape[0]
    half_words = xs_ref.shape[1]
    cw = wgu_ref.shape[1]
    hw = cw // 2

    @pl.when(c == 0)
    def _():
        words = xs_ref[...]
        x_ref[:, :half_words] = pltpu.bitcast(words << 16, F32).astype(BF16)
        x_ref[:, half_words:] = pltpu.bitcast(words & jnp.uint32(HI16), F32).astype(BF16)
        for sb in range(bm // sub_rows):
            y_ref[sb * sub_rows:(sb + 1) * sub_rows, :] = jnp.broadcast_to(
                jnp.where(sb < n_sub, bd_ref[...], 0.0), (sub_rows, y_ref.shape[1]))

    def token_rows(m_rows):
        wgu = wgu_ref[...].astype(BF16)
        lo = pltpu.bitcast(wd_ref[:hw // 2, :].astype(BF16).astype(F32), U32) >> 16
        hi = pltpu.bitcast(wd_ref[hw // 2:, :].astype(BF16).astype(F32), U32) & jnp.uint32(HI16)
        wdi = pltpu.bitcast(lo | hi, BF16)
        rows = slice(0, m_rows)
        even = (lax.broadcasted_iota(I32, (m_rows, V7X_LANES), 1) & 1) == 0
        gu = jnp.dot(x_ref[rows, :], wgu, preferred_element_type=F32) + bgu_ref[...]
        gs = jnp.minimum(gu, SWIGLU_LIMIT)
        gate = gs * jax.nn.sigmoid(SWIGLU_ALPHA * gs)
        up = jnp.clip(gu, -SWIGLU_LIMIT, SWIGLU_LIMIT) + 1.0
        prod = []
        for v in range(cw // V7X_LANES):
            cols = slice(v * V7X_LANES, (v + 1) * V7X_LANES)
            prod.append(gate[:, cols] * pltpu.roll(up[:, cols], V7X_LANES - 1, 1))
        nv = hw // V7X_LANES
        act = jnp.concatenate(
            [jnp.where(even, prod[v], pltpu.roll(prod[nv + v], 1, 1)) for v in range(nv)], axis=1)
        y_ref[rows, :] += jnp.dot(act.astype(BF16), wdi, preferred_element_type=F32)

    for k in range(1, bm // sub_rows + 1):
        @pl.when(n_sub == k)
        def _(k=k):
            token_rows(k * sub_rows)


def _experts(xs, tile_expert, tile_subs, n_used, w_gu, w_d, b_gu, b_d, *, layer, bm, sub_rows):
    n_slots, dw = xs.shape
    depth, n_exp, d, f2 = w_gu.shape
    cw = _tile(f2, MOE_GU_CHUNK)
    assert cw % (2 * V7X_LANES) == 0
    nt, nc = n_slots // bm, f2 // cw

    def tile_idx(i, nu):
        return jnp.minimum(i, nu[0] - 1)

    def chunk_idx(i, c, nu):
        return jnp.where(i < nu[0], c, nc - 1)

    def expert(i, te, nu):
        return te[tile_idx(i, nu)]

    grid_spec = pltpu.PrefetchScalarGridSpec(
        num_scalar_prefetch=3,
        grid=(nt, nc),
        in_specs=[
            pl.BlockSpec((bm, dw), lambda i, c, te, ns, nu: (tile_idx(i, nu), 0)),
            pl.BlockSpec((None, None, d, cw),
                         lambda i, c, te, ns, nu: (layer, expert(i, te, nu), 0, chunk_idx(i, c, nu))),
            pl.BlockSpec((None, None, cw // 2, d),
                         lambda i, c, te, ns, nu: (layer, expert(i, te, nu), chunk_idx(i, c, nu), 0)),
            pl.BlockSpec((None, None, 1, cw),
                         lambda i, c, te, ns, nu: (layer, expert(i, te, nu), 0, chunk_idx(i, c, nu))),
            pl.BlockSpec((None, None, 1, d), lambda i, c, te, ns, nu: (layer, expert(i, te, nu), 0, 0)),
        ],
        out_specs=pl.BlockSpec((bm, d), lambda i, c, te, ns, nu: (i, 0)),
        scratch_shapes=[pltpu.VMEM((bm, d), BF16)],
    )
    return pl.pallas_call(
        functools.partial(_experts_kernel, sub_rows=sub_rows),
        grid_spec=grid_spec,
        out_shape=jax.ShapeDtypeStruct((n_slots, d), F32),
        compiler_params=_params(("arbitrary", "arbitrary"), VMEM_LIMIT_EXPERTS_BYTES),
        name="moe_experts",
    )(tile_expert, tile_subs, n_used, xs, w_gu, w_d,
      b_gu.reshape(depth, n_exp, 1, f2), b_d.reshape(depth, n_exp, 1, d))


def _combine_kernel(dest_ref, x_ref, r_ref, y_ref, *rest, final_norm):
    if final_norm:
        g_ref, o_ref, buf_ref, sem = rest
    else:
        o_ref, buf_ref, sem = rest
    tm = x_ref.shape[0]

    def row_copy(r, k):
        return pltpu.make_async_copy(y_ref.at[pl.ds(dest_ref[r * TOP_K + k], 1), :],
                                     buf_ref.at[k, pl.ds(r, 1), :], sem)

    def issue(r, carry):
        for k in range(TOP_K):
            row_copy(r, k).start(priority=k % 2)
        return carry

    def drain(r, carry):
        for k in range(TOP_K):
            row_copy(r, k).wait()
        return carry

    lax.fori_loop(0, tm, issue, 0)
    lax.fori_loop(0, tm, drain, 0)

    piece = _tile(tm, COMBINE_SUM_ROWS)
    for start in range(0, tm, piece):
        rows = slice(start, start + piece)
        acc = x_ref[rows, :]
        for k in range(TOP_K):
            acc = acc + r_ref[rows, 2 * TOP_K + k:2 * TOP_K + k + 1] * buf_ref[k, rows, :]
        if final_norm:
            acc = acc * _rms_scale(acc) * g_ref[...]
        o_ref[rows, :] = acc


def _combine(x, r, y, dest_flat, final_g=None):
    t, d = x.shape
    tm = _tile(t, MOE_TOKEN_TILE)
    in_specs = [
        pl.BlockSpec((tm * TOP_K,), lambda i: (i,), memory_space=pltpu.SMEM),
        pl.BlockSpec((tm, d), lambda i: (i, 0)),
        pl.BlockSpec((tm, V7X_LANES), lambda i: (i, 0)),
        pl.BlockSpec(memory_space=pl.ANY),
    ]
    args = [dest_flat, x, r, y]
    if final_g is not None:
        in_specs.append(pl.BlockSpec((1, d), lambda i: (0, 0)))
        args.append(final_g.reshape(1, d))
    return pl.pallas_call(
        functools.partial(_combine_kernel, final_norm=final_g is not None),
        grid=(t // tm,),
        in_specs=in_specs,
        out_specs=pl.BlockSpec((tm, d), lambda i: (i, 0)),
        out_shape=jax.ShapeDtypeStruct((t, d), F32),
        scratch_shapes=[pltpu.VMEM((TOP_K, tm, d), F32), pltpu.SemaphoreType.DMA(())],
        compiler_params=_params(("arbitrary",)),
        name="moe_combine",
    )(*args)


def _moe(x, norm_g, router_w, router_b, w_gate_up, b_gate_up, w_down, b_down, *, layer, final_g=None):
    t, d = x.shape
    n_exp = router_w.shape[1]
    bm = MOE_ROW_TILE
    sub_rows = _tile(bm, MOE_SUB_ROWS)
    n_tiles = t * TOP_K // bm + n_exp
    hp, r, cnt = _router(x, norm_g, router_w, router_b)

    top_i = r[:, :TOP_K].astype(I32)
    rank = r[:, TOP_K:2 * TOP_K].astype(I32)
    counts = cnt[0].astype(I32)
    exp_tiles = (counts + bm - 1) // bm
    tile_end = jnp.cumsum(exp_tiles)
    tile_start = tile_end - exp_tiles
    per_tile = jnp.maximum((counts + exp_tiles * sub_rows - 1) // jnp.maximum(exp_tiles * sub_rows, 1), 1) * sub_rows
    tok_fill = per_tile[top_i]
    tok_tile = jnp.floor((rank.astype(F32) + 0.5) / tok_fill.astype(F32)).astype(I32)
    dest_flat = ((tile_start[top_i] + tok_tile) * bm + rank - tok_tile * tok_fill).reshape(-1)
    n_used = tile_end[-1:].astype(I32)
    tile_ids = jnp.arange(n_tiles, dtype=I32)
    tile_expert = jnp.minimum(jnp.sum(tile_end[None, :] <= tile_ids[:, None], axis=1), n_exp - 1).astype(I32)
    tile_rows = jnp.clip(counts[tile_expert] - (tile_ids - tile_start[tile_expert]) * per_tile[tile_expert],
                         0, per_tile[tile_expert])
    tile_subs = jnp.where(tile_ids < n_used[0], (tile_rows + sub_rows - 1) // sub_rows, 0).astype(I32)

    xs = _dispatch(hp, dest_flat, n_tiles * bm)
    y = _experts(xs, tile_expert, tile_subs, n_used, w_gate_up, w_down, b_gate_up, b_down,
                 layer=layer, bm=bm, sub_rows=sub_rows)
    return _combine(x, r, y, dest_flat, final_g)


def _rope_tables(seq, dh, rot_dim, q_scale):
    half = rot_dim // 2
    inv_freq = ROPE_THETA ** (-jnp.arange(0, rot_dim, 2, dtype=F32) / rot_dim)
    ang = jnp.arange(seq, dtype=F32)[:, None] * inv_freq[None, :]
    cos, sin = jnp.cos(ang), jnp.sin(ang)
    ones = jnp.ones((seq, dh - rot_dim), F32)
    zeros_rest = jnp.zeros((seq, dh - rot_dim), F32)
    zeros_half = jnp.zeros((seq, half), F32)
    c = jnp.concatenate([cos, cos, ones], axis=1)
    s1 = jnp.concatenate([zeros_half, sin, zeros_rest], axis=1)
    s2 = jnp.concatenate([-sin, zeros_half, zeros_rest], axis=1)
    k_tab = jnp.stack([jnp.tile(a, (1, 2)) for a in (c, s1, s2)])
    v_tab = jnp.stack([jnp.ones_like(k_tab[0]), jnp.zeros_like(k_tab[0]), jnp.zeros_like(k_tab[0])])
    return jnp.stack([k_tab * q_scale, k_tab, v_tab])


def _gmlp_layer(x, norm_g, w_in, ln_g, ln_b, w_s, b_s, w_out):
    z = _norm_matmul(x, norm_g, w_in.astype(BF16))
    y = _gmlp_gate(z, ln_g, ln_b, w_s, b_s)
    return _matmul_residual(y, w_out.astype(BF16), x)


def _attn_layer(x, norm_g, w_qkv, lq1, lk1, lq2, lk2, subln_g, w_o, lambda_init, *, batch, seq):
    d = x.shape[1]
    dh = lq1.shape[0]
    n_heads = d // (2 * dh)
    rot_dim = dh // 4
    tab = _rope_tables(seq, dh, rot_dim, dh ** -0.5)
    qkv = _norm_matmul(x, norm_g, w_qkv.astype(BF16), rope_tab=tab, seq=seq, rot_half=rot_dim // 2)
    lam = jnp.exp(jnp.sum(lq1 * lk1)) - jnp.exp(jnp.sum(lq2 * lk2)) + lambda_init
    o = _diff_attention(qkv, lam.astype(F32), subln_g, batch=batch, seq=seq, n_heads=n_heads,
                        out_scale=1.0 - lambda_init)
    return _matmul_residual(o, w_o.astype(BF16), x)


def kernel(x, attn_norm_g, ffn_norm_g, final_norm_g, a_w_in, a_ln_g, a_ln_b, a_w_s, a_b_s, a_w_out,
           b_w_qkv, b_lambda_q1, b_lambda_k1, b_lambda_q2, b_lambda_k2, b_subln_g, b_w_o,
           router_w, router_b, w_gate_up, b_gate_up, w_down, b_down):
    batch, seq, d = x.shape
    depth = attn_norm_g.shape[0]
    n_mixers = 2
    h = x.reshape(batch * seq, d)
    for i in range(depth):
        j = i // n_mixers
        if i % n_mixers == 0:
            h = _gmlp_layer(h, attn_norm_g[i], a_w_in[j], a_ln_g[j], a_ln_b[j], a_w_s[j], a_b_s[j], a_w_out[j])
        else:
            lambda_init = 0.8 - 0.6 * math.exp(-0.3 * i)
            h = _attn_layer(h, attn_norm_g[i], b_w_qkv[j], b_lambda_q1[j], b_lambda_k1[j], b_lambda_q2[j],
                            b_lambda_k2[j], b_subln_g[j], b_w_o[j], lambda_init, batch=batch, seq=seq)
        h = _moe(h, ffn_norm_g[i], router_w[i], router_b[i], w_gate_up, b_gate_up, w_down, b_down,
                 layer=i, final_g=final_norm_g if i == depth - 1 else None)
    return h.reshape(batch, seq, d)
```

```python
import functools
import math

import jax
import jax.numpy as jnp
from jax import lax
from jax.experimental import pallas as pl
from jax.experimental.pallas import tpu as pltpu

F32 = jnp.float32
BF16 = jnp.bfloat16
I32 = jnp.int32
U32 = jnp.uint32

EPS = 1e-5
TOP_K = 4
ROPE_THETA = 500000.0
SWIGLU_LIMIT = 7.0
SWIGLU_ALPHA = 1.702

V7X_LANES = 128
V7X_VMEM_BYTES = 64 * 1024 * 1024
VMEM_LIMIT_BYTES = V7X_VMEM_BYTES * 7 // 8
VMEM_LIMIT_EXPERTS_BYTES = V7X_VMEM_BYTES * 15 // 16
HI16 = 0xFFFF0000

ATTN_Q_TILE = 512
MOE_TOKEN_TILE = 512
COMBINE_SUM_ROWS = 128
MOE_ROW_TILE = 1024
MOE_SUB_ROWS = 128
MOE_GU_CHUNK = 1024


def _tile(dim, want):
    t = min(dim, want)
    while dim % t:
        t //= 2
    return t


def _params(semantics, vmem_limit_bytes=VMEM_LIMIT_BYTES):
    return pltpu.CompilerParams(dimension_semantics=semantics, vmem_limit_bytes=vmem_limit_bytes)


def _rms_scale(x):
    return lax.rsqrt(jnp.mean(x * x, axis=-1, keepdims=True) + EPS)


def _norm_matmul_gelu_kernel(x_ref, g_ref, w_ref, o_ref, xn_ref):
    @pl.when(pl.program_id(1) == 0)
    def _():
        x = x_ref[...]
        xn_ref[...] = (x * _rms_scale(x) * g_ref[...]).astype(BF16)

    y = jnp.dot(xn_ref[...], w_ref[...], preferred_element_type=F32)
    o_ref[...] = jax.nn.gelu(y).astype(o_ref.dtype)


def _norm_matmul_rope_kernel(x_ref, g_ref, w_ref, tab_ref, o_ref, xn_ref, *, shift):
    @pl.when(pl.program_id(1) == 0)
    def _():
        x = x_ref[...]
        xn_ref[...] = (x * _rms_scale(x) * g_ref[...]).astype(BF16)

    y = jnp.dot(xn_ref[...], w_ref[...], preferred_element_type=F32)
    c, s1, s2 = tab_ref[0], tab_ref[1], tab_ref[2]
    for h in range(y.shape[1] // V7X_LANES):
        cols = slice(h * V7X_LANES, (h + 1) * V7X_LANES)
        yh = y[:, cols]
        r = (yh * c + pltpu.roll(yh, shift, 1) * s1
             + pltpu.roll(yh, V7X_LANES - shift, 1) * s2)
        o_ref[:, cols] = r.astype(o_ref.dtype)


def _norm_matmul(x, g, w, *, rope_tab=None, seq=None, rot_half=0):
    t, k = x.shape
    n = w.shape[1]
    tm = _tile(t if seq is None else seq, 1024)
    tn = _tile(n if rope_tab is None else n // 3, 1024)
    grid = (t // tm, n // tn)
    in_specs = [
        pl.BlockSpec((tm, k), lambda i, j: (i, 0)),
        pl.BlockSpec((1, k), lambda i, j: (0, 0)),
        pl.BlockSpec((k, tn), lambda i, j: (0, j)),
    ]
    args = [x, g.reshape(1, k), w]
    if rope_tab is None:
        body = _norm_matmul_gelu_kernel
    else:
        assert n % (3 * tn) == 0 and tn % V7X_LANES == 0
        section_tiles = n // (3 * tn)
        s_tiles = seq // tm
        in_specs.append(pl.BlockSpec(
            (None, 3, tm, V7X_LANES), lambda i, j: (j // section_tiles, 0, i % s_tiles, 0)))
        args.append(rope_tab)
        body = functools.partial(_norm_matmul_rope_kernel, shift=rot_half)
    return pl.pallas_call(
        body,
        grid=grid,
        in_specs=in_specs,
        out_specs=pl.BlockSpec((tm, tn), lambda i, j: (i, j)),
        out_shape=jax.ShapeDtypeStruct((t, n), BF16),
        scratch_shapes=[pltpu.VMEM((tm, k), BF16)],
        compiler_params=_params(("parallel", "arbitrary")),
        name="norm_matmul_gelu" if rope_tab is None else "norm_matmul_rope",
    )(*args)


def _matmul_residual_kernel(a_ref, w_ref, r_ref, o_ref):
    o_ref[...] = r_ref[...] + jnp.dot(a_ref[...], w_ref[...], preferred_element_type=F32)


def _matmul_residual(a, w, res):
    t, k = a.shape
    n = w.shape[1]
    tm, tn = _tile(t, 512), _tile(n, 2048)
    return pl.pallas_call(
        _matmul_residual_kernel,
        grid=(t // tm, n // tn),
        in_specs=[
            pl.BlockSpec((tm, k), lambda i, j: (i, 0)),
            pl.BlockSpec((k, tn), lambda i, j: (0, j)),
            pl.BlockSpec((tm, tn), lambda i, j: (i, j)),
        ],
        out_specs=pl.BlockSpec((tm, tn), lambda i, j: (i, j)),
        out_shape=jax.ShapeDtypeStruct((t, n), F32),
        compiler_params=_params(("parallel", "parallel")),
        name="matmul_residual",
    )(a, w, res)


def _gmlp_gate_kernel(u_ref, v_ref, lng_ref, lnb_ref, ws_ref, bst_ref, y_ref, vn_ref):
    n_groups, chunk, _ = ws_ref.shape
    tm, width = u_ref.shape
    gd = width // n_groups
    v = v_ref[...].astype(F32)
    vc = v - jnp.mean(v, axis=-1, keepdims=True)
    inv = lax.rsqrt(jnp.mean(vc * vc, axis=-1, keepdims=True) + EPS)
    vn_ref[...] = (vc * inv * lng_ref[...] + lnb_ref[...]).astype(BF16)
    causal = (lax.broadcasted_iota(I32, (chunk, chunk), 0) >= lax.broadcasted_iota(I32, (chunk, chunk), 1))
    for g in range(n_groups):
        w = jnp.where(causal, ws_ref[g], 0.0).astype(BF16)
        bias = bst_ref[:, g:g + 1]
        cols = slice(g * gd, (g + 1) * gd)
        for ci in range(tm // chunk):
            rows = slice(ci * chunk, (ci + 1) * chunk)
            sv = jnp.dot(w, vn_ref[rows, cols], preferred_element_type=F32) + bias
            y_ref[rows, cols] = (u_ref[rows, cols].astype(F32) * sv).astype(y_ref.dtype)


def _gmlp_gate(z, ln_g, ln_b, w_s, b_s):
    t, w2 = z.shape
    width = w2 // 2
    n_groups, chunk, _ = w_s.shape
    tm = _tile(t, 4 * chunk)
    return pl.pallas_call(
        _gmlp_gate_kernel,
        grid=(t // tm,),
        in_specs=[
            pl.BlockSpec((tm, width), lambda i: (i, 0)),
            pl.BlockSpec((tm, width), lambda i: (i, 1)),
            pl.BlockSpec((1, width), lambda i: (0, 0)),
            pl.BlockSpec((1, width), lambda i: (0, 0)),
            pl.BlockSpec((n_groups, chunk, chunk), lambda i: (0, 0, 0)),
            pl.BlockSpec((chunk, n_groups), lambda i: (0, 0)),
        ],
        out_specs=pl.BlockSpec((tm, width), lambda i: (i, 0)),
        out_shape=jax.ShapeDtypeStruct((t, width), BF16),
        scratch_shapes=[pltpu.VMEM((tm, width), BF16)],
        compiler_params=_params(("parallel",)),
        name="gmlp_gate",
    )(z, z, ln_g.reshape(1, width), ln_b.reshape(1, width), w_s, b_s.T)


def _diff_attn_kernel(lam_ref, q_ref, k_ref, v_ref, g_ref, o_ref, qs_ref, sw_ref, sn_ref, m_ref, acc_ref, *,
                      out_scale):
    qi = pl.program_id(2)
    tq, dh2 = q_ref.shape
    half = dh2 // 2
    n_wide = qi // 2
    has_odd = qi % 2 == 1

    q = q_ref[...]
    lane = lax.broadcasted_iota(I32, q.shape, 1)
    qs_ref[:tq, :] = jnp.where(lane < half, q, jnp.zeros_like(q))
    qs_ref[tq:, :] = jnp.where(lane >= half, q, jnp.zeros_like(q))
    m_ref[...] = jnp.full_like(m_ref, -jnp.inf)
    acc_ref[...] = jnp.zeros_like(acc_ref)

    def kv_rows(index, width):
        return pl.ds(pl.multiple_of(index * width, width), width)

    def scores(s_ref, slot, index):
        width = s_ref.shape[2]
        s_ref[slot] = lax.dot_general(qs_ref[...], k_ref[kv_rows(index, width), :], (((1,), (1,)), ((), ())),
                                      preferred_element_type=F32)

    def softmax_pv(s_ref, slot, index, masked=False):
        width = s_ref.shape[2]
        v = v_ref[kv_rows(index, width), :]
        v1 = jnp.concatenate([v, jnp.ones_like(v)], axis=1)
        for r in range(2):
            rs = slice(r * tq, (r + 1) * tq)
            s = s_ref[slot, rs, :]
            if masked:
                row = lax.broadcasted_iota(I32, s.shape, 0)
                col = lax.broadcasted_iota(I32, s.shape, 1)
                s = jnp.where(col <= row, s, -jnp.inf)
            m_old = m_ref[rs, :]
            m_new = jnp.maximum(m_old, jnp.max(s, axis=-1, keepdims=True))
            p = jnp.exp((s - m_new).astype(BF16))
            alpha = jnp.exp(m_old - m_new)
            acc_ref[rs, :] = alpha * acc_ref[rs, :] + jnp.dot(p, v1, preferred_element_type=F32)
            m_ref[rs, :] = m_new

    @pl.when(n_wide > 0)
    def _():
        scores(sw_ref, 0, 0)

    def wide_step(w, carry):
        slot = w % 2
        softmax_pv(sw_ref, slot, w)
        scores(sw_ref, 1 - slot, w + 1)
        return carry

    lax.fori_loop(0, n_wide - 1, wide_step, 0)
    last_wide = n_wide - 1

    @pl.when(jnp.logical_and(n_wide > 0, has_odd))
    def _():
        scores(sn_ref, 0, qi - 1)
        softmax_pv(sw_ref, last_wide % 2, last_wide)
        scores(sn_ref, 1, qi)
        softmax_pv(sn_ref, 0, qi - 1)
        softmax_pv(sn_ref, 1, qi, masked=True)

    @pl.when(jnp.logical_and(n_wide > 0, jnp.logical_not(has_odd)))
    def _():
        scores(sn_ref, 0, qi)
        softmax_pv(sw_ref, last_wide % 2, last_wide)
        softmax_pv(sn_ref, 0, qi, masked=True)

    @pl.when(jnp.logical_and(n_wide == 0, has_odd))
    def _():
        scores(sn_ref, 0, qi - 1)
        scores(sn_ref, 1, qi)
        softmax_pv(sn_ref, 0, qi - 1)
        softmax_pv(sn_ref, 1, qi, masked=True)

    @pl.when(jnp.logical_and(n_wide == 0, jnp.logical_not(has_odd)))
    def _():
        scores(sn_ref, 0, qi)
        softmax_pv(sn_ref, 0, qi, masked=True)

    o = acc_ref[:, :dh2] / acc_ref[:, dh2:]
    o = o[:tq] - lam_ref[0] * o[tq:]
    o = o * _rms_scale(o) * g_ref[...] * out_scale
    o_ref[...] = o.astype(o_ref.dtype)


def _diff_attention(qkv, lam, subln_g, *, batch, seq, n_heads, out_scale):
    t = qkv.shape[0]
    dh2 = qkv.shape[1] // (3 * n_heads)
    tq = _tile(seq, ATTN_Q_TILE)
    nq = seq // tq
    return pl.pallas_call(
        functools.partial(_diff_attn_kernel, out_scale=out_scale),
        grid=(batch, n_heads, nq),
        in_specs=[
            pl.BlockSpec(memory_space=pltpu.SMEM),
            pl.BlockSpec((tq, dh2), lambda b, h, i: (b * nq + i, h)),
            pl.BlockSpec((seq, dh2), lambda b, h, i: (b, n_heads + h)),
            pl.BlockSpec((seq, dh2), lambda b, h, i: (b, 2 * n_heads + h)),
            pl.BlockSpec((1, dh2), lambda b, h, i: (0, 0)),
        ],
        out_specs=pl.BlockSpec((tq, dh2), lambda b, h, i: (b * nq + i, h)),
        out_shape=jax.ShapeDtypeStruct((t, n_heads * dh2), BF16),
        scratch_shapes=[
            pltpu.VMEM((2 * tq, dh2), BF16),
            pltpu.VMEM((2, 2 * tq, 2 * tq), F32),
            pltpu.VMEM((2, 2 * tq, tq), F32),
            pltpu.VMEM((2 * tq, 1), F32),
            pltpu.VMEM((2 * tq, 2 * dh2), F32),
        ],
        compiler_params=_params(("parallel", "parallel", "arbitrary")),
        name="diff_attention",
    )(lam.reshape(1), qkv, qkv, qkv, subln_g.reshape(1, dh2))


def _router_kernel(x_ref, g_ref, rw_ref, rb_ref, hp_ref, r_ref, cnt_ref, slots_ref, carry_ref):
    @pl.when(pl.program_id(0) == 0)
    def _():
        carry_ref[...] = jnp.zeros_like(carry_ref)

    slots_ref[...] = jnp.zeros_like(slots_ref)

    x = x_ref[...]
    tm, d = x.shape
    n_exp = rw_ref.shape[1]
    h = x * _rms_scale(x) * g_ref[...]
    h_hi = h.astype(BF16)
    h_hi32 = h_hi.astype(F32)
    half = d // 2
    lo_word = pltpu.bitcast(h_hi32[:, :half], U32) >> 16
    hi_word = pltpu.bitcast(h_hi32[:, half:], U32) & jnp.uint32(HI16)
    hp_ref[...] = lo_word | hi_word

    h_lo = (h - h_hi32).astype(BF16)
    rw = rw_ref[...]
    w_hi = rw.astype(BF16)
    w_lo = (rw - w_hi.astype(F32)).astype(BF16)
    logits = (jnp.dot(h_hi, w_hi, preferred_element_type=F32)
              + jnp.dot(h_lo, w_hi, preferred_element_type=F32)
              + jnp.dot(h_hi, w_lo, preferred_element_type=F32)) + rb_ref[...]

    e_iota = lax.broadcasted_iota(I32, (tm, n_exp), 1).astype(F32)
    work = logits
    vals, onehots, idxs = [], [], []
    for _ in range(TOP_K):
        m = jnp.max(work, axis=-1, keepdims=True)
        idx = jnp.min(jnp.where(work == m, e_iota, float(n_exp)), axis=-1, keepdims=True)
        hit = e_iota == idx
        vals.append(m)
        idxs.append(idx)
        onehots.append(hit)
        work = jnp.where(hit, -jnp.inf, work)
    exps = [jnp.exp(v - vals[0]) for v in vals]
    denom = functools.reduce(lambda a, b: a + b, exps)

    chosen = functools.reduce(jnp.logical_or, onehots)
    chosen_f = jnp.where(chosen, 1.0, 0.0)
    earlier = (lax.broadcasted_iota(I32, (tm, tm), 0) > lax.broadcasted_iota(I32, (tm, tm), 1))
    before = jnp.dot(jnp.where(earlier, 1.0, 0.0).astype(BF16), chosen_f.astype(BF16),
                     preferred_element_type=F32) + carry_ref[...]
    carry_ref[...] = carry_ref[...] + jnp.sum(chosen_f, axis=0, keepdims=True)
    cnt_ref[...] = carry_ref[...]

    lane = lax.broadcasted_iota(I32, r_ref.shape, 1)
    out = jnp.zeros(r_ref.shape, F32)
    for k in range(TOP_K):
        rank = jnp.sum(jnp.where(onehots[k], before, 0.0), axis=-1, keepdims=True)
        out = jnp.where(lane == k, idxs[k], out)
        out = jnp.where(lane == TOP_K + k, rank, out)
        out = jnp.where(lane == 2 * TOP_K + k, exps[k] / denom, out)
    r_ref[...] = out


def _router(x, g, router_w, router_b, n_slots):
    t, d = x.shape
    n_exp = router_w.shape[1]
    tm = _tile(t, 512)
    steps = t // tm
    assert n_slots % (8 * steps) == 0
    return pl.pallas_call(
        _router_kernel,
        grid=(t // tm,),
        in_specs=[
            pl.BlockSpec((tm, d), lambda i: (i, 0)),
            pl.BlockSpec((1, d), lambda i: (0, 0)),
            pl.BlockSpec((d, n_exp), lambda i: (0, 0)),
            pl.BlockSpec((1, n_exp), lambda i: (0, 0)),
        ],
        out_specs=[
            pl.BlockSpec((tm, d // 2), lambda i: (i, 0)),
            pl.BlockSpec((tm, V7X_LANES), lambda i: (i, 0)),
            pl.BlockSpec((1, n_exp), lambda i: (0, 0)),
            pl.BlockSpec((n_slots // steps, d // 2), lambda i: (i, 0)),
        ],
        out_shape=[
            jax.ShapeDtypeStruct((t, d // 2), U32),
            jax.ShapeDtypeStruct((t, V7X_LANES), F32),
            jax.ShapeDtypeStruct((1, n_exp), F32),
            jax.ShapeDtypeStruct((n_slots, d // 2), U32),
        ],
        scratch_shapes=[pltpu.VMEM((1, n_exp), F32)],
        compiler_params=_params(("arbitrary",)),
        name="moe_router",
    )(x, g.reshape(1, d), router_w, router_b.reshape(1, n_exp))


def _dispatch_kernel(dest_ref, hp_ref, xs_init_ref, xs_ref, sem):
    del xs_init_ref
    tm = hp_ref.shape[0]

    def row_copy(r, k):
        return pltpu.make_async_copy(hp_ref.at[pl.ds(r, 1), :],
                                     xs_ref.at[pl.ds(dest_ref[r * TOP_K + k], 1), :], sem)

    def issue(r, carry):
        for k in range(TOP_K):
            row_copy(r, k).start(priority=k % 2)
        return carry

    def drain(r, carry):
        for k in range(TOP_K):
            row_copy(r, k).wait()
        return carry

    lax.fori_loop(0, tm, issue, 0)
    lax.fori_loop(0, tm, drain, 0)


def _dispatch(hp, dest_flat, xs_init):
    t, dw = hp.shape
    n_slots = xs_init.shape[0]
    tm = _tile(t, MOE_TOKEN_TILE)
    return pl.pallas_call(
        _dispatch_kernel,
        grid=(t // tm,),
        in_specs=[
            pl.BlockSpec((tm * TOP_K,), lambda i: (i,), memory_space=pltpu.SMEM),
            pl.BlockSpec((tm, dw), lambda i: (i, 0)),
            pl.BlockSpec(memory_space=pl.ANY),
        ],
        out_specs=pl.BlockSpec(memory_space=pl.ANY),
        out_shape=jax.ShapeDtypeStruct((n_slots, dw), U32),
        scratch_shapes=[pltpu.SemaphoreType.DMA(())],
        input_output_aliases={2: 0},
        compiler_params=_params(("arbitrary",)),
        name="moe_dispatch",
    )(dest_flat, hp, xs_init)


def _experts_kernel(te_ref, ns_ref, nu_ref, xs_ref, wgu_ref, wd_ref, bgu_ref, bd_ref, y_ref, x_ref, *, sub_rows):
    del te_ref, nu_ref
    i, c = pl.program_id(0), pl.program_id(1)
    n_sub = ns_ref[i]
    bm = y_ref.shape[0]
    half_words = xs_ref.shape[1]
    cw = wgu_ref.shape[1]
    hw = cw // 2

    @pl.when(c == 0)
    def _():
        words = xs_ref[...]
        x_ref[:, :half_words] = pltpu.bitcast(words << 16, F32).astype(BF16)
        x_ref[:, half_words:] = pltpu.bitcast(words & jnp.uint32(HI16), F32).astype(BF16)
        for sb in range(bm // sub_rows):
            y_ref[sb * sub_rows:(sb + 1) * sub_rows, :] = jnp.broadcast_to(
                jnp.where(sb < n_sub, bd_ref[...], 0.0), (sub_rows, y_ref.shape[1]))

    def token_rows(m_rows):
        wgu = wgu_ref[...].astype(BF16)
        lo = pltpu.bitcast(wd_ref[:hw // 2, :].astype(BF16).astype(F32), U32) >> 16
        hi = pltpu.bitcast(wd_ref[hw // 2:, :].astype(BF16).astype(F32), U32) & jnp.uint32(HI16)
        wdi = pltpu.bitcast(lo | hi, BF16)
        rows = slice(0, m_rows)
        even = (lax.broadcasted_iota(I32, (m_rows, V7X_LANES), 1) & 1) == 0
        gu = jnp.dot(x_ref[rows, :], wgu, preferred_element_type=F32) + bgu_ref[...]
        gs = jnp.minimum(gu, SWIGLU_LIMIT)
        gate = gs * jax.nn.sigmoid(SWIGLU_ALPHA * gs)
        up = jnp.clip(gu, -SWIGLU_LIMIT, SWIGLU_LIMIT) + 1.0
        prod = []
        for v in range(cw // V7X_LANES):
            cols = slice(v * V7X_LANES, (v + 1) * V7X_LANES)
            prod.append(gate[:, cols] * pltpu.roll(up[:, cols], V7X_LANES - 1, 1))
        nv = hw // V7X_LANES
        act = jnp.concatenate(
            [jnp.where(even, prod[v], pltpu.roll(prod[nv + v], 1, 1)) for v in range(nv)], axis=1)
        y_ref[rows, :] += jnp.dot(act.astype(BF16), wdi, preferred_element_type=F32)

    for k in range(1, bm // sub_rows + 1):
        @pl.when(n_sub == k)
        def _(k=k):
            token_rows(k * sub_rows)


def _experts(xs, tile_expert, tile_subs, n_used, w_gu, w_d, b_gu, b_d, *, layer, bm, sub_rows):
    n_slots, dw = xs.shape
    depth, n_exp, d, f2 = w_gu.shape
    cw = _tile(f2, MOE_GU_CHUNK)
    assert cw % (2 * V7X_LANES) == 0
    nt, nc = n_slots // bm, f2 // cw

    def tile_idx(i, nu):
        return jnp.minimum(i, nu[0] - 1)

    def chunk_idx(i, c, nu):
        return jnp.where(i < nu[0], c, nc - 1)

    def expert(i, te, nu):
        return te[tile_idx(i, nu)]

    grid_spec = pltpu.PrefetchScalarGridSpec(
        num_scalar_prefetch=3,
        grid=(nt, nc),
        in_specs=[
            pl.BlockSpec((bm, dw), lambda i, c, te, ns, nu: (tile_idx(i, nu), 0)),
            pl.BlockSpec((None, None, d, cw),
                         lambda i, c, te, ns, nu: (layer, expert(i, te, nu), 0, chunk_idx(i, c, nu))),
            pl.BlockSpec((None, None, cw // 2, d),
                         lambda i, c, te, ns, nu: (layer, expert(i, te, nu), chunk_idx(i, c, nu), 0)),
            pl.BlockSpec((None, None, 1, cw),
                         lambda i, c, te, ns, nu: (layer, expert(i, te, nu), 0, chunk_idx(i, c, nu))),
            pl.BlockSpec((None, None, 1, d), lambda i, c, te, ns, nu: (layer, expert(i, te, nu), 0, 0)),
        ],
        out_specs=pl.BlockSpec((bm, d), lambda i, c, te, ns, nu: (i, 0)),
        scratch_shapes=[pltpu.VMEM((bm, d), BF16)],
    )
    return pl.pallas_call(
        functools.partial(_experts_kernel, sub_rows=sub_rows),
        grid_spec=grid_spec,
        out_shape=jax.ShapeDtypeStruct((n_slots, d), F32),
        compiler_params=_params(("arbitrary", "arbitrary"), VMEM_LIMIT_EXPERTS_BYTES),
        name="moe_experts",
    )(tile_expert, tile_subs, n_used, xs, w_gu, w_d,
      b_gu.reshape(depth, n_exp, 1, f2), b_d.reshape(depth, n_exp, 1, d))


def _combine_kernel(dest_ref, x_ref, r_ref, y_ref, *rest, final_norm):
    if final_norm:
        g_ref, o_ref, buf_ref, sem = rest
    else:
        o_ref, buf_ref, sem = rest
    tm = x_ref.shape[0]

    def row_copy(r, k):
        return pltpu.make_async_copy(y_ref.at[pl.ds(dest_ref[r * TOP_K + k], 1), :],
                                     buf_ref.at[k, pl.ds(r, 1), :], sem)

    def issue(r, carry):
        for k in range(TOP_K):
            row_copy(r, k).start(priority=k % 2)
        return carry

    def drain(r, carry):
        for k in range(TOP_K):
            row_copy(r, k).wait()
        return carry

    lax.fori_loop(0, tm, issue, 0)
    lax.fori_loop(0, tm, drain, 0)

    piece = _tile(tm, COMBINE_SUM_ROWS)
    for start in range(0, tm, piece):
        rows = slice(start, start + piece)
        acc = x_ref[rows, :]
        for k in range(TOP_K):
            acc = acc + r_ref[rows, 2 * TOP_K + k:2 * TOP_K + k + 1] * buf_ref[k, rows, :]
        if final_norm:
            acc = acc * _rms_scale(acc) * g_ref[...]
        o_ref[rows, :] = acc


def _combine(x, r, y, dest_flat, final_g=None):
    t, d = x.shape
    tm = _tile(t, MOE_TOKEN_TILE)
    in_specs = [
        pl.BlockSpec((tm * TOP_K,), lambda i: (i,), memory_space=pltpu.SMEM),
        pl.BlockSpec((tm, d), lambda i: (i, 0)),
        pl.BlockSpec((tm, V7X_LANES), lambda i: (i, 0)),
        pl.BlockSpec(memory_space=pl.ANY),
    ]
    args = [dest_flat, x, r, y]
    if final_g is not None:
        in_specs.append(pl.BlockSpec((1, d), lambda i: (0, 0)))
        args.append(final_g.reshape(1, d))
    return pl.pallas_call(
        functools.partial(_combine_kernel, final_norm=final_g is not None),
        grid=(t // tm,),
        in_specs=in_specs,
        out_specs=pl.BlockSpec((tm, d), lambda i: (i, 0)),
        out_shape=jax.ShapeDtypeStruct((t, d), F32),
        scratch_shapes=[pltpu.VMEM((TOP_K, tm, d), F32), pltpu.SemaphoreType.DMA(())],
        compiler_params=_params(("arbitrary",)),
        name="moe_combine",
    )(*args)


def _moe(x, norm_g, router_w, router_b, w_gate_up, b_gate_up, w_down, b_down, *, layer, final_g=None):
    t, d = x.shape
    n_exp = router_w.shape[1]
    bm = MOE_ROW_TILE
    sub_rows = _tile(bm, MOE_SUB_ROWS)
    n_tiles = t * TOP_K // bm + n_exp
    hp, r, cnt, xs_zero = _router(x, norm_g, router_w, router_b, n_tiles * bm)

    def by_expert(table, idx):
        hit = idx[..., None] == jnp.arange(n_exp, dtype=I32)
        return jnp.sum(jnp.where(hit, table, 0), axis=-1)

    top_i = r[:, :TOP_K].astype(I32)
    rank = r[:, TOP_K:2 * TOP_K].astype(I32)
    counts = cnt[0].astype(I32)
    exp_tiles = (counts + bm - 1) // bm
    tile_end = jnp.cumsum(exp_tiles)
    tile_start = tile_end - exp_tiles
    per_tile = jnp.maximum((counts + exp_tiles * sub_rows - 1) // jnp.maximum(exp_tiles * sub_rows, 1), 1) * sub_rows
    tok_fill = by_expert(per_tile, top_i)
    tok_tile = jnp.floor((rank.astype(F32) + 0.5) / tok_fill.astype(F32)).astype(I32)
    dest_flat = ((by_expert(tile_start, top_i) + tok_tile) * bm + rank - tok_tile * tok_fill).reshape(-1)
    n_used = tile_end[-1:].astype(I32)
    tile_ids = jnp.arange(n_tiles, dtype=I32)
    tile_expert = jnp.minimum(jnp.sum(tile_end[None, :] <= tile_ids[:, None], axis=1), n_exp - 1).astype(I32)
    tile_rows = jnp.clip(counts[tile_expert] - (tile_ids - tile_start[tile_expert]) * per_tile[tile_expert],
                         0, per_tile[tile_expert])
    tile_subs = jnp.where(tile_ids < n_used[0], (tile_rows + sub_rows - 1) // sub_rows, 0).astype(I32)

    xs = _dispatch(hp, dest_flat, xs_zero)
    y = _experts(xs, tile_expert, tile_subs, n_used, w_gate_up, w_down, b_gate_up, b_down,
                 layer=layer, bm=bm, sub_rows=sub_rows)
    return _combine(x, r, y, dest_flat, final_g)


def _rope_tables(seq, dh, rot_dim, q_scale):
    half = rot_dim // 2
    inv_freq = ROPE_THETA ** (-jnp.arange(0, rot_dim, 2, dtype=F32) / rot_dim)
    ang = jnp.arange(seq, dtype=F32)[:, None] * inv_freq[None, :]
    cos, sin = jnp.cos(ang), jnp.sin(ang)
    ones = jnp.ones((seq, dh - rot_dim), F32)
    zeros_rest = jnp.zeros((seq, dh - rot_dim), F32)
    zeros_half = jnp.zeros((seq, half), F32)
    c = jnp.concatenate([cos, cos, ones], axis=1)
    s1 = jnp.concatenate([zeros_half, sin, zeros_rest], axis=1)
    s2 = jnp.concatenate([-sin, zeros_half, zeros_rest], axis=1)
    k_tab = jnp.stack([jnp.tile(a, (1, 2)) for a in (c, s1, s2)])
    v_tab = jnp.stack([jnp.ones_like(k_tab[0]), jnp.zeros_like(k_tab[0]), jnp.zeros_like(k_tab[0])])
    return jnp.stack([k_tab * q_scale, k_tab, v_tab])


def _gmlp_layer(x, norm_g, w_in, ln_g, ln_b, w_s, b_s, w_out):
    z = _norm_matmul(x, norm_g, w_in.astype(BF16))
    y = _gmlp_gate(z, ln_g, ln_b, w_s, b_s)
    return _matmul_residual(y, w_out.astype(BF16), x)


def _attn_layer(x, norm_g, w_qkv, lq1, lk1, lq2, lk2, subln_g, w_o, lambda_init, *, batch, seq):
    d = x.shape[1]
    dh = lq1.shape[0]
    n_heads = d // (2 * dh)
    rot_dim = dh // 4
    tab = _rope_tables(seq, dh, rot_dim, dh ** -0.5)
    qkv = _norm_matmul(x, norm_g, w_qkv.astype(BF16), rope_tab=tab, seq=seq, rot_half=rot_dim // 2)
    lam = jnp.exp(jnp.sum(lq1 * lk1)) - jnp.exp(jnp.sum(lq2 * lk2)) + lambda_init
    o = _diff_attention(qkv, lam.astype(F32), subln_g, batch=batch, seq=seq, n_heads=n_heads,
                        out_scale=1.0 - lambda_init)
    return _matmul_residual(o, w_o.astype(BF16), x)


def kernel(x, attn_norm_g, ffn_norm_g, final_norm_g, a_w_in, a_ln_g, a_ln_b, a_w_s, a_b_s, a_w_out,
           b_w_qkv, b_lambda_q1, b_lambda_k1, b_lambda_q2, b_lambda_k2, b_subln_g, b_w_o,
           router_w, router_b, w_gate_up, b_gate_up, w_down, b_down):
    batch, seq, d = x.shape
    depth = attn_norm_g.shape[0]
    n_mixers = 2
    h = x.reshape(batch * seq, d)
    for i in range(depth):
        j = i // n_mixers
        if i % n_mixers == 0:
            h = _gmlp_layer(h, attn_norm_g[i], a_w_in[j], a_ln_g[j], a_ln_b[j], a_w_s[j], a_b_s[j], a_w_out[j])
        else:
            lambda_init = 0.8 - 0.6 * math.exp(-0.3 * i)
            h = _attn_layer(h, attn_norm_g[i], b_w_qkv[j], b_lambda_q1[j], b_lambda_k1[j], b_lambda_q2[j],
                            b_lambda_k2[j], b_subln_g[j], b_w_o[j], lambda_init, batch=batch, seq=seq)
        h = _moe(h, ffn_norm_g[i], router_w[i], router_b[i], w_gate_up, b_gate_up, w_down, b_down,
                 layer=i, final_g=final_norm_g if i == depth - 1 else None)
    return h.reshape(batch, seq, d)
```
